```python
import jax, jax.numpy as jnp
from jax import lax
import numpy as np

D_MODEL = 1024
BATCH = 8
SEQ = 2048
DEPTH = 4
DEC_BATCH = 128
DEC_SEQ = 8
PAST_LEN = 16384
PAGE_SIZE = 128

D_A = D_MODEL // 2
D_B = D_MODEL // 2
CONV_A = 31
CONV_B = 3
D_IN_EVEN = 2 * D_A + 3 * D_B
D_C = D_MODEL
C_HEADS = 8
C_HEAD_DIM = D_C // C_HEADS
CHUNK = 128
D_FF = 4 * D_MODEL
N_EVEN = (DEPTH + 1) // 2
N_ODD = DEPTH // 2
N_MOD = 6
EPS = 1e-6

kernel_name = 'hybrid_conformerconv_shortconv_gmlp_decoder_step'


def rmsnorm(x, g):
    xf = x.astype(jnp.float32)
    y = xf * lax.rsqrt(jnp.mean(xf * xf, axis=-1, keepdims=True) + EPS)
    return (y * g.astype(jnp.float32)).astype(x.dtype)


def layernorm(x, g, b):
    xf = x.astype(jnp.float32)
    mu = jnp.mean(xf, axis=-1, keepdims=True)
    var = jnp.mean(jnp.square(xf - mu), axis=-1, keepdims=True)
    y = (xf - mu) * lax.rsqrt(var + EPS) * g.astype(jnp.float32) + b.astype(jnp.float32)
    return y.astype(x.dtype)


def modulate(x, shift, scale):
    return x * (1 + scale[:, None, :]) + shift[:, None, :]


def causal_dwconv(x_ext, w):
    ch = x_ext.shape[-1]
    return lax.conv_general_dilated(
        x_ext, w[:, None, :].astype(x_ext.dtype), window_strides=(1,), padding='VALID',
        dimension_numbers=('NWC', 'WIO', 'NWC'), feature_group_count=ch)


def even_mixer(h, hist_a, hist_b, w_in, conv_a_w, conv_a_b, ln_a_g, ln_a_b, conv_b_w, w_out):
    z = h @ w_in
    a_val, a_gate, b_x, b_b, b_c = jnp.split(
        z, [D_A, 2 * D_A, 2 * D_A + D_B, 2 * D_A + 2 * D_B], axis=-1)
    a = a_val * jax.nn.sigmoid(a_gate)
    a_ext = jnp.concatenate([hist_a, a], axis=1)
    a = causal_dwconv(a_ext, conv_a_w) + conv_a_b
    a = jax.nn.silu(layernorm(a, ln_a_g, ln_a_b))
    bx = b_c * b_x
    b_ext = jnp.concatenate([hist_b, bx], axis=1)
    b = b_b * causal_dwconv(b_ext, conv_b_w)
    y = jnp.concatenate([a, b], axis=-1) @ w_out
    return y, a_ext[:, -(CONV_A - 1):], b_ext[:, -(CONV_B - 1):]


def odd_mixer(h, w_in, b_in, ln_v_g, ln_v_b, w_s, b_s, w_out):
    n, length, _ = h.shape
    z = jax.nn.gelu(h @ w_in + b_in)
    u, v = jnp.split(z, 2, axis=-1)
    v = layernorm(v, ln_v_g, ln_v_b)
    t = min(length, CHUNK)
    n_chunks = length // t
    mask = jnp.tril(jnp.ones((t, t), dtype=bool))
    ws = jnp.where(mask, w_s[:, :t, :t], 0).astype(v.dtype)
    vc = v.reshape(n, n_chunks, t, C_HEADS, C_HEAD_DIM)
    s = jnp.einsum('hts,bnshd->bnthd', ws, vc) + b_s[:, :t].T[None, None, :, :, None]
    s = s.reshape(n, length, D_C)
    y = (u * s) @ w_out
    return y, v[:, -t:]


def trunk(x, c, hist_a, hist_b, w_in_ab, conv_a_w, conv_a_b, ln_a_g, ln_a_b, conv_b_w, w_out_ab,
          w_in_c, b_in_c, ln_v_g, ln_v_b, w_s, b_s, w_out_c, w_ada, b_ada, norm_g, w_ff1, w_ff2, final_g):
    n = x.shape[0]
    c_act = jax.nn.silu(c)
    new_a, new_b, new_v = [], [], []
    for l in range(DEPTH):
        mod = (c_act @ w_ada[l] + b_ada[l]).reshape(n, N_MOD, D_MODEL)
        sh1, sc1, g1, sh2, sc2, g2 = (mod[:, i] for i in range(N_MOD))
        h = modulate(rmsnorm(x, norm_g[l, 0]), sh1, sc1)
        if l % 2 == 0:
            e = l // 2
            y, sa, sb = even_mixer(h, hist_a[e], hist_b[e], w_in_ab[e], conv_a_w[e], conv_a_b[e],
                                   ln_a_g[e], ln_a_b[e], conv_b_w[e], w_out_ab[e])
            new_a.append(sa)
            new_b.append(sb)
        else:
            o = l // 2
            y, sv = odd_mixer(h, w_in_c[o], b_in_c[o], ln_v_g[o], ln_v_b[o], w_s[o], b_s[o], w_out_c[o])
            new_v.append(sv)
        x = x + g1[:, None, :] * y
        h = modulate(rmsnorm(x, norm_g[l, 1]), sh2, sc2)
        x = x + g2[:, None, :] * (jnp.square(jax.nn.relu(h @ w_ff1[l])) @ w_ff2[l])
    x = rmsnorm(x, final_g)
    return x, jnp.stack(new_a), jnp.stack(new_b), jnp.stack(new_v)


def setup_inputs(seed: int = 0) -> dict:
    key = jax.random.key(seed)
    ks = iter(list(jax.random.split(key, 32)))

    def nrm(shape, scale):
        return jax.random.normal(next(ks), shape, jnp.float32) * scale

    d = D_MODEL
    return {
        'x_prompt': nrm((BATCH, SEQ, d), 1.0),
        'x_sample': nrm((DEC_BATCH, DEC_SEQ, d), 1.0),
        'state_conv_a': nrm((N_EVEN, DEC_BATCH, CONV_A - 1, D_A), 0.5),
        'state_conv_b': nrm((N_EVEN, DEC_BATCH, CONV_B - 1, D_B), 0.5),
        'c_prompt': nrm((BATCH, d), 1.0),
        'c_sample': nrm((DEC_BATCH, d), 1.0),
        'w_in_ab': nrm((N_EVEN, d, D_IN_EVEN), d ** -0.5),
        'conv_a_w': nrm((N_EVEN, CONV_A, D_A), CONV_A ** -0.5),
        'conv_a_b': nrm((N_EVEN, D_A), 0.02),
        'ln_a_g': 1.0 + nrm((N_EVEN, D_A), 0.02),
        'ln_a_b': nrm((N_EVEN, D_A), 0.02),
        'conv_b_w': nrm((N_EVEN, CONV_B, D_B), CONV_B ** -0.5),
        'w_out_ab': nrm((N_EVEN, D_A + D_B, d), (D_A + D_B) ** -0.5),
        'w_in_c': nrm((N_ODD, d, 2 * D_C), d ** -0.5),
        'b_in_c': nrm((N_ODD, 2 * D_C), 0.02),
        'ln_v_g': 1.0 + nrm((N_ODD, D_C), 0.02),
        'ln_v_b': nrm((N_ODD, D_C), 0.02),
        'w_s': nrm((N_ODD, C_HEADS, CHUNK, CHUNK), CHUNK ** -0.5),
        'b_s': 1.0 + nrm((N_ODD, C_HEADS, CHUNK), 0.02),
        'w_out_c': nrm((N_ODD, D_C, d), D_C ** -0.5),
        'w_ada': nrm((DEPTH, d, N_MOD * d), 0.3 * d ** -0.5),
        'b_ada': nrm((DEPTH, N_MOD * d), 0.05),
        'norm_g': 1.0 + nrm((DEPTH, 2, d), 0.02),
        'w_ff1': nrm((DEPTH, d, D_FF), d ** -0.5),
        'w_ff2': nrm((DEPTH, D_FF, d), D_FF ** -0.5),
        'final_g': 1.0 + nrm((d,), 0.02),
    }


def reference(x_prompt, x_sample, state_conv_a, state_conv_b, c_prompt, c_sample,
              w_in_ab, conv_a_w, conv_a_b, ln_a_g, ln_a_b, conv_b_w, w_out_ab,
              w_in_c, b_in_c, ln_v_g, ln_v_b, w_s, b_s, w_out_c,
              w_ada, b_ada, norm_g, w_ff1, w_ff2, final_g):
    weights = (w_in_ab, conv_a_w, conv_a_b, ln_a_g, ln_a_b, conv_b_w, w_out_ab,
               w_in_c, b_in_c, ln_v_g, ln_v_b, w_s, b_s, w_out_c,
               w_ada, b_ada, norm_g, w_ff1, w_ff2, final_g)
    hist_a0 = jnp.zeros((N_EVEN, x_prompt.shape[0], CONV_A - 1, D_A), x_prompt.dtype)
    hist_b0 = jnp.zeros((N_EVEN, x_prompt.shape[0], CONV_B - 1, D_B), x_prompt.dtype)
    y_prompt, conv_a_p, conv_b_p, chunk_v_p = trunk(x_prompt, c_prompt, hist_a0, hist_b0, *weights)
    y_sample, conv_a_s, conv_b_s, chunk_v_s = trunk(
        x_sample, c_sample, state_conv_a.astype(x_sample.dtype), state_conv_b.astype(x_sample.dtype), *weights)
    return (y_prompt, y_sample, conv_a_p, conv_a_s, conv_b_p, conv_b_s, chunk_v_p, chunk_v_s)
```

```python
import functools

import jax
import jax.numpy as jnp
from jax import lax
from jax.experimental import pallas as pl
from jax.experimental.pallas import tpu as pltpu

D_MODEL = 1024
DEPTH = 4
D_A = 512
D_B = 512
CONV_A = 31
CONV_B = 3
D_IN_EVEN = 2 * D_A + 3 * D_B
D_C = 1024
C_HEADS = 8
C_HEAD_DIM = 128
CHUNK = 128
D_FF = 4096
N_MOD = 6
EPS = 1e-6

BF16 = jnp.bfloat16
F32 = jnp.float32

VMEM_LIMIT_BYTES = 56 * 1024 * 1024
SUBLANES = 8
ROW_TILE = 512
SEQ_BLOCK = 32
FF_CHUNK = 1024
A_HIST_ROWS = 32
B_HIST_ROWS = 8


def _params():
    return pltpu.CompilerParams(
        dimension_semantics=("arbitrary", "arbitrary"),
        vmem_limit_bytes=VMEM_LIMIT_BYTES)


def _resident(shape, index_map):
    return pl.BlockSpec(shape, index_map, pipeline_mode=pl.Buffered(1))


def _rms(x, g):
    ms = jnp.mean(x * x, axis=-1, keepdims=True)
    return x * lax.rsqrt(ms + EPS) * g


def _rms_mod(x, g, shift, scale):
    return _rms(x, g) * (1.0 + scale) + shift


def _layernorm(x, g, b):
    mu = jnp.mean(x, axis=-1, keepdims=True)
    xc = x - mu
    var = jnp.mean(xc * xc, axis=-1, keepdims=True)
    return xc * lax.rsqrt(var + EPS) * g + b


def _silu(x):
    return x * jax.nn.sigmoid(x)


def _dot(a, b):
    return jnp.dot(a, b, preferred_element_type=F32)


def _ada_kernel(c_ref, w_ref, b_ref, o_ref):
    c = c_ref[...]
    o_ref[...] = _dot(_silu(c).astype(BF16), w_ref[...].astype(BF16)) + b_ref[...]


def _ada_call(c_all, w_ada, b_ada):
    n = c_all.shape[0]
    b4 = b_ada.reshape(DEPTH, N_MOD, 1, D_MODEL)
    return pl.pallas_call(
        _ada_kernel,
        grid=(DEPTH, N_MOD),
        in_specs=[
            pl.BlockSpec((n, D_MODEL), lambda l, j: (0, 0)),
            pl.BlockSpec((None, D_MODEL, D_MODEL), lambda l, j: (l, 0, j)),
            pl.BlockSpec((None, None, 1, D_MODEL), lambda l, j: (l, j, 0, 0)),
        ],
        out_specs=pl.BlockSpec((None, None, n, D_MODEL), lambda l, j: (l, j, 0, 0)),
        out_shape=jax.ShapeDtypeStruct((DEPTH, N_MOD, n, D_MODEL), F32),
        compiler_params=_params(),
        name="adaln",
    )(c_all, w_ada, b4)


def _mlp_body(x2d, h_bf, gate, w1_ref, w2_ref):
    acc = None
    for j in range(D_FF // FF_CHUNK):
        cols = slice(j * FF_CHUNK, (j + 1) * FF_CHUNK)
        hid = _dot(h_bf, w1_ref[:, cols])
        hid = jnp.square(jnp.maximum(hid, 0.0)).astype(BF16)
        part = _dot(hid, w2_ref[cols, :])
        acc = part if acc is None else acc + part
    return x2d + gate * acc


def _mlp_prompt_kernel(x_ref, mod_ref, g_ref, w1_ref, w2_ref, fg_ref, o_ref, *, final):
    b = pl.program_id(0)
    x = x_ref[...]
    shift = mod_ref[3, pl.ds(b, 1), :]
    scale = mod_ref[4, pl.ds(b, 1), :]
    gate = mod_ref[5, pl.ds(b, 1), :]
    h = _rms_mod(x, g_ref[1:2, :], shift, scale).astype(BF16)
    y = _mlp_body(x, h, gate, w1_ref, w2_ref)
    if final:
        y = _rms(y, fg_ref[...])
    o_ref[...] = y


def _mlp_sample_kernel(x_ref, mod_ref, g_ref, w1_ref, w2_ref, fg_ref, o_ref, *, final):
    x = x_ref[...]
    t, s, d = x.shape
    h = _rms_mod(x, g_ref[1:2, :], mod_ref[3][None], mod_ref[4][None])
    h = h.reshape(t * s, d).astype(BF16)
    gate = jnp.broadcast_to(mod_ref[5][None], (t, s, d)).reshape(t * s, d)
    y = _mlp_body(x.reshape(t * s, d), h, gate, w1_ref, w2_ref)
    if final:
        y = _rms(y, fg_ref[...])
    o_ref[...] = y.reshape(t, s, d)


def _mlp_prompt_call(x, mod, norm_g, w1, w2, final_g, l, final):
    nb, seq, d = x.shape
    n_prompt_blk = (mod.shape[2] - nb) // nb
    return pl.pallas_call(
        functools.partial(_mlp_prompt_kernel, final=final),
        grid=(nb, seq // ROW_TILE),
        in_specs=[
            pl.BlockSpec((None, ROW_TILE, d), lambda b, t: (b, t, 0)),
            pl.BlockSpec((None, N_MOD, nb, d), lambda b, t: (l, 0, n_prompt_blk, 0)),
            pl.BlockSpec((None, 2, d), lambda b, t: (l, 0, 0)),
            _resident((None, d, D_FF), lambda b, t: (l, 0, 0)),
            _resident((None, D_FF, d), lambda b, t: (l, 0, 0)),
            pl.BlockSpec((1, d), lambda b, t: (0, 0)),
        ],
        out_specs=pl.BlockSpec((None, ROW_TILE, d), lambda b, t: (b, t, 0)),
        out_shape=jax.ShapeDtypeStruct(x.shape, F32),
        compiler_params=_params(),
        name=f"mlp_prompt_{l}",
    )(x, mod, norm_g, w1, w2, final_g)


def _mlp_sample_call(x, mod, norm_g, w1, w2, final_g, l, final):
    t, ns, d = x.shape
    return pl.pallas_call(
        functools.partial(_mlp_sample_kernel, final=final),
        grid=(ns // SEQ_BLOCK, 1),
        in_specs=[
            pl.BlockSpec((t, SEQ_BLOCK, d), lambda s, _: (0, s, 0)),
            pl.BlockSpec((None, N_MOD, SEQ_BLOCK, d), lambda s, _: (l, 0, s, 0)),
            pl.BlockSpec((None, 2, d), lambda s, _: (l, 0, 0)),
            _resident((None, d, D_FF), lambda s, _: (l, 0, 0)),
            _resident((None, D_FF, d), lambda s, _: (l, 0, 0)),
            pl.BlockSpec((1, d), lambda s, _: (0, 0)),
        ],
        out_specs=pl.BlockSpec((t, SEQ_BLOCK, d), lambda s, _: (0, s, 0)),
        out_shape=jax.ShapeDtypeStruct(x.shape, F32),
        compiler_params=_params(),
        name=f"mlp_sample_{l}",
    )(x, mod, norm_g, w1, w2, final_g)


def _even_prompt_kernel(x_ref, mod_ref, g_ref, w_in_ref, caw_ref, cab_ref, lng_ref, lnb_ref,
                        cbw_ref, w_out_ref, o_ref, sa_ref, sb_ref, abuf, bbuf):
    b = pl.program_id(0)
    t = pl.program_id(1)
    rows = x_ref.shape[0]

    @pl.when(t == 0)
    def _():
        abuf[0:A_HIST_ROWS, :] = jnp.zeros((A_HIST_ROWS, D_A), F32)
        bbuf[0:B_HIST_ROWS, :] = jnp.zeros((B_HIST_ROWS, D_B), F32)

    x = x_ref[...]
    shift = mod_ref[0, pl.ds(b, 1), :]
    scale = mod_ref[1, pl.ds(b, 1), :]
    gate = mod_ref[2, pl.ds(b, 1), :]
    h = _rms_mod(x, g_ref[0:1, :], shift, scale).astype(BF16)
    z = _dot(h, w_in_ref[...])

    abuf[A_HIST_ROWS:A_HIST_ROWS + rows, :] = z[:, 0:D_A] * jax.nn.sigmoid(z[:, D_A:2 * D_A])
    first_a = A_HIST_ROWS - (CONV_A - 1)
    acc = jnp.broadcast_to(cab_ref[...], (rows, D_A))
    for k in range(CONV_A):
        acc = acc + caw_ref[k:k + 1, :] * abuf[pl.ds(first_a + k, rows), :]
    a_out = _silu(_layernorm(acc, lng_ref[...], lnb_ref[...]))

    bbuf[B_HIST_ROWS:B_HIST_ROWS + rows, :] = z[:, 4 * D_A:5 * D_A] * z[:, 2 * D_A:3 * D_A]
    first_b = B_HIST_ROWS - (CONV_B - 1)
    accb = cbw_ref[0:1, :] * bbuf[pl.ds(first_b, rows), :]
    for k in range(1, CONV_B):
        accb = accb + cbw_ref[k:k + 1, :] * bbuf[pl.ds(first_b + k, rows), :]
    b_out = z[:, 3 * D_A:4 * D_A] * accb

    y = _dot(jnp.concatenate([a_out, b_out], axis=-1).astype(BF16), w_out_ref[...])
    o_ref[...] = x + gate * y

    @pl.when(t == pl.num_programs(1) - 1)
    def _():
        sa_ref[...] = abuf[pl.ds(rows + first_a, CONV_A - 1), :]
        sb_ref[...] = bbuf[pl.ds(rows + first_b, CONV_B - 1), :]

    abuf[0:A_HIST_ROWS, :] = abuf[rows:rows + A_HIST_ROWS, :]
    bbuf[0:B_HIST_ROWS, :] = bbuf[rows:rows + B_HIST_ROWS, :]


def _even_prompt_call(x, mod, norm_g, w_in, caw, cab, lng, lnb, cbw, w_out, l):
    nb, seq, d = x.shape
    e = l // 2
    n_prompt_blk = (mod.shape[2] - nb) // nb
    vec = lambda width: pl.BlockSpec((None, 1, width), lambda b, t: (e, 0, 0))
    return pl.pallas_call(
        _even_prompt_kernel,
        grid=(nb, seq // ROW_TILE),
        in_specs=[
            pl.BlockSpec((None, ROW_TILE, d), lambda b, t: (b, t, 0)),
            pl.BlockSpec((None, N_MOD, nb, d), lambda b, t: (l, 0, n_prompt_blk, 0)),
            pl.BlockSpec((None, 2, d), lambda b, t: (l, 0, 0)),
            _resident((None, d, D_IN_EVEN), lambda b, t: (e, 0, 0)),
            pl.BlockSpec((None, CONV_A, D_A), lambda b, t: (e, 0, 0)),
            vec(D_A), vec(D_A), vec(D_A),
            pl.BlockSpec((None, CONV_B, D_B), lambda b, t: (e, 0, 0)),
            _resident((None, D_A + D_B, d), lambda b, t: (e, 0, 0)),
        ],
        out_specs=[
            pl.BlockSpec((None, ROW_TILE, d), lambda b, t: (b, t, 0)),
            pl.BlockSpec((None, CONV_A - 1, D_A), lambda b, t: (b, 0, 0)),
            pl.BlockSpec((None, CONV_B - 1, D_B), lambda b, t: (b, 0, 0)),
        ],
        out_shape=[
            jax.ShapeDtypeStruct(x.shape, F32),
            jax.ShapeDtypeStruct((nb, CONV_A - 1, D_A), F32),
            jax.ShapeDtypeStruct((nb, CONV_B - 1, D_B), F32),
        ],
        scratch_shapes=[
            pltpu.VMEM((ROW_TILE + A_HIST_ROWS, D_A), F32),
            pltpu.VMEM((ROW_TILE + B_HIST_ROWS, D_B), F32),
        ],
        compiler_params=_params(),
        name=f"even_prompt_{l}",
    )(x, mod, norm_g, w_in, caw, cab, lng, lnb, cbw, w_out)


def _even_sample_kernel(x_ref, mod_ref, g_ref, ha_ref, hb_ref, w_in_ref, caw_ref, cab_ref,
                        lng_ref, lnb_ref, cbw_ref, w_out_ref, o_ref, sa_ref, sb_ref):
    x = x_ref[...]
    t, s, d = x.shape
    h = _rms_mod(x, g_ref[0:1, :], mod_ref[0][None], mod_ref[1][None])
    z = _dot(h.reshape(t * s, d).astype(BF16), w_in_ref[...]).reshape(t, s, D_IN_EVEN)

    a = z[:, :, 0:D_A] * jax.nn.sigmoid(z[:, :, D_A:2 * D_A])
    a_ext = jnp.concatenate([ha_ref[...], a], axis=0)
    acc = jnp.broadcast_to(cab_ref[...][None], (t, s, D_A))
    for k in range(CONV_A):
        acc = acc + caw_ref[k:k + 1, :][None] * a_ext[k:k + t]
    a_out = _silu(_layernorm(acc, lng_ref[...][None], lnb_ref[...][None]))
    sa_ref[...] = a_ext[t:]

    bx = z[:, :, 4 * D_A:5 * D_A] * z[:, :, 2 * D_A:3 * D_A]
    b_ext = jnp.concatenate([hb_ref[...], bx], axis=0)
    accb = cbw_ref[0:1, :][None] * b_ext[0:t]
    for k in range(1, CONV_B):
        accb = accb + cbw_ref[k:k + 1, :][None] * b_ext[k:k + t]
    b_out = z[:, :, 3 * D_A:4 * D_A] * accb
    sb_ref[...] = b_ext[t:]

    cat = jnp.concatenate([a_out, b_out], axis=-1).reshape(t * s, D_A + D_B).astype(BF16)
    y = _dot(cat, w_out_ref[...]).reshape(t, s, d)
    o_ref[...] = x + mod_ref[2][None] * y


def _even_sample_call(x, mod, norm_g, hist_a, hist_b, w_in, caw, cab, lng, lnb, cbw, w_out, l):
    t, ns, d = x.shape
    e = l // 2
    vec = lambda width: pl.BlockSpec((None, 1, width), lambda s, _: (e, 0, 0))
    return pl.pallas_call(
        _even_sample_kernel,
        grid=(ns // SEQ_BLOCK, 1),
        in_specs=[
            pl.BlockSpec((t, SEQ_BLOCK, d), lambda s, _: (0, s, 0)),
            pl.BlockSpec((None, N_MOD, SEQ_BLOCK, d), lambda s, _: (l, 0, s, 0)),
            pl.BlockSpec((None, 2, d), lambda s, _: (l, 0, 0)),
            pl.BlockSpec((None, CONV_A - 1, SEQ_BLOCK, D_A), lambda s, _: (e, 0, s, 0)),
            pl.BlockSpec((None, CONV_B - 1, SEQ_BLOCK, D_B), lambda s, _: (e, 0, s, 0)),
            _resident((None, d, D_IN_EVEN), lambda s, _: (e, 0, 0)),
            pl.BlockSpec((None, CONV_A, D_A), lambda s, _: (e, 0, 0)),
            vec(D_A), vec(D_A), vec(D_A),
            pl.BlockSpec((None, CONV_B, D_B), lambda s, _: (e, 0, 0)),
            _resident((None, D_A + D_B, d), lambda s, _: (e, 0, 0)),
        ],
        out_specs=[
            pl.BlockSpec((t, SEQ_BLOCK, d), lambda s, _: (0, s, 0)),
            pl.BlockSpec((CONV_A - 1, SEQ_BLOCK, D_A), lambda s, _: (0, s, 0)),
            pl.BlockSpec((CONV_B - 1, SEQ_BLOCK, D_B), lambda s, _: (0, s, 0)),
        ],
        out_shape=[
            jax.ShapeDtypeStruct(x.shape, F32),
            jax.ShapeDtypeStruct((CONV_A - 1, ns, D_A), F32),
            jax.ShapeDtypeStruct((CONV_B - 1, ns, D_B), F32),
        ],
        compiler_params=_params(),
        name=f"even_sample_{l}",
    )(x, mod, norm_g, hist_a, hist_b, w_in, caw, cab, lng, lnb, cbw, w_out)


def _odd_prompt_kernel(x_ref, mod_ref, g_ref, w_in_ref, b_in_ref, lng_ref, lnb_ref, ws_ref,
                       sbias_ref, w_out_ref, o_ref, cv_ref):
    b = pl.program_id(0)
    t = pl.program_id(1)
    rows = x_ref.shape[0]
    x = x_ref[...]
    shift = mod_ref[0, pl.ds(b, 1), :]
    scale = mod_ref[1, pl.ds(b, 1), :]
    gate = mod_ref[2, pl.ds(b, 1), :]
    h = _rms_mod(x, g_ref[0:1, :], shift, scale).astype(BF16)
    z = jax.nn.gelu(_dot(h, w_in_ref[...]) + b_in_ref[...])
    u = z[:, 0:D_C]
    v = _layernorm(z[:, D_C:2 * D_C], lng_ref[...], lnb_ref[...])

    @pl.when(t == pl.num_programs(1) - 1)
    def _():
        cv_ref[...] = v[rows - CHUNK:rows, :]

    v_bf = v.astype(BF16)
    causal = (lax.broadcasted_iota(jnp.int32, (CHUNK, CHUNK), 0)
              >= lax.broadcasted_iota(jnp.int32, (CHUNK, CHUNK), 1))
    ws = [jnp.where(causal, ws_ref[hd], 0.0).astype(BF16) for hd in range(C_HEADS)]
    s_rows = []
    for c in range(rows // CHUNK):
        heads = [
            _dot(ws[hd], v_bf[c * CHUNK:(c + 1) * CHUNK, hd * C_HEAD_DIM:(hd + 1) * C_HEAD_DIM])
            for hd in range(C_HEADS)
        ]
        s_rows.append(jnp.concatenate(heads, axis=-1) + sbias_ref[...])
    s = jnp.concatenate(s_rows, axis=0)
    y = _dot((u * s).astype(BF16), w_out_ref[...])
    o_ref[...] = x + gate * y


def _odd_prompt_call(x, mod, norm_g, w_in, b_in, lng, lnb, w_s, sbias, w_out, l):
    nb, seq, d = x.shape
    o = l // 2
    n_prompt_blk = (mod.shape[2] - nb) // nb
    return pl.pallas_call(
        _odd_prompt_kernel,
        grid=(nb, seq // ROW_TILE),
        in_specs=[
            pl.BlockSpec((None, ROW_TILE, d), lambda b, t: (b, t, 0)),
            pl.BlockSpec((None, N_MOD, nb, d), lambda b, t: (l, 0, n_prompt_blk, 0)),
            pl.BlockSpec((None, 2, d), lambda b, t: (l, 0, 0)),
            _resident((None, d, 2 * D_C), lambda b, t: (o, 0, 0)),
            pl.BlockSpec((None, 1, 2 * D_C), lambda b, t: (o, 0, 0)),
            pl.BlockSpec((None, 1, D_C), lambda b, t: (o, 0, 0)),
            pl.BlockSpec((None, 1, D_C), lambda b, t: (o, 0, 0)),
            pl.BlockSpec((None, C_HEADS, CHUNK, CHUNK), lambda b, t: (o, 0, 0, 0)),
            pl.BlockSpec((None, CHUNK, D_C), lambda b, t: (o, 0, 0)),
            _resident((None, D_C, d), lambda b, t: (o, 0, 0)),
        ],
        out_specs=[
            pl.BlockSpec((None, ROW_TILE, d), lambda b, t: (b, t, 0)),
            pl.BlockSpec((None, CHUNK, D_C), lambda b, t: (b, 0, 0)),
        ],
        out_shape=[
            jax.ShapeDtypeStruct(x.shape, F32),
            jax.ShapeDtypeStruct((nb, CHUNK, D_C), F32),
        ],
        compiler_params=_params(),
        name=f"odd_prompt_{l}",
    )(x, mod, norm_g, w_in, b_in, lng, lnb, w_s, sbias, w_out)


def _odd_sample_kernel(x_ref, mod_ref, g_ref, w_in_ref, b_in_ref, lng_ref, lnb_ref, wm_ref,
                       sbias_ref, w_out_ref, o_ref, cv_ref):
    x = x_ref[...]
    t, s, d = x.shape
    h = _rms_mod(x, g_ref[0:1, :], mod_ref[0][None], mod_ref[1][None])
    z = jax.nn.gelu(_dot(h.reshape(t * s, d).astype(BF16), w_in_ref[...]) + b_in_ref[...])
    z = z.reshape(t, s, 2 * D_C)
    u = z[:, :, 0:D_C]
    v = _layernorm(z[:, :, D_C:2 * D_C], lng_ref[...][None], lnb_ref[...][None])
    cv_ref[...] = v
    gated = []
    for i in range(t):
        s_i = jnp.broadcast_to(sbias_ref[i:i + 1, :], (s, D_C))
        for j in range(i + 1):
            s_i = s_i + wm_ref[i, j:j + 1, :] * v[j]
        gated.append(u[i] * s_i)
    us = jnp.stack(gated, axis=0).reshape(t * s, D_C).astype(BF16)
    y = _dot(us, w_out_ref[...]).reshape(t, s, d)
    o_ref[...] = x + mod_ref[2][None] * y


def _odd_sample_call(x, mod, norm_g, w_in, b_in, lng, lnb, wm, sbias, w_out, l):
    t, ns, d = x.shape
    o = l // 2
    return pl.pallas_call(
        _odd_sample_kernel,
        grid=(ns // SEQ_BLOCK, 1),
        in_specs=[
            pl.BlockSpec((t, SEQ_BLOCK, d), lambda s, _: (0, s, 0)),
            pl.BlockSpec((None, N_MOD, SEQ_BLOCK, d), lambda s, _: (l, 0, s, 0)),
            pl.BlockSpec((None, 2, d), lambda s, _: (l, 0, 0)),
            _resident((None, d, 2 * D_C), lambda s, _: (o, 0, 0)),
            pl.BlockSpec((None, 1, 2 * D_C), lambda s, _: (o, 0, 0)),
            pl.BlockSpec((None, 1, D_C), lambda s, _: (o, 0, 0)),
            pl.BlockSpec((None, 1, D_C), lambda s, _: (o, 0, 0)),
            pl.BlockSpec((None, t, t, D_C), lambda s, _: (o, 0, 0, 0)),
            pl.BlockSpec((None, t, D_C), lambda s, _: (o, 0, 0)),
            _resident((None, D_C, d), lambda s, _: (o, 0, 0)),
        ],
        out_specs=[
            pl.BlockSpec((t, SEQ_BLOCK, d), lambda s, _: (0, s, 0)),
            pl.BlockSpec((t, SEQ_BLOCK, D_C), lambda s, _: (0, s, 0)),
        ],
        out_shape=[
            jax.ShapeDtypeStruct(x.shape, F32),
            jax.ShapeDtypeStruct((t, ns, D_C), F32),
        ],
        compiler_params=_params(),
        name=f"odd_sample_{l}",
    )(x, mod, norm_g, w_in, b_in, lng, lnb, wm, sbias, w_out)


def kernel(x_prompt, x_sample, state_conv_a, state_conv_b, c_prompt, c_sample, w_in_ab, conv_a_w, conv_a_b, ln_a_g, ln_a_b, conv_b_w, w_out_ab, w_in_c, b_in_c, ln_v_g, ln_v_b, w_s, b_s, w_out_c, w_ada, b_ada, norm_g, w_ff1, w_ff2, final_g):
    dec_seq = x_sample.shape[1]
    n_even, n_odd = w_in_ab.shape[0], w_in_c.shape[0]

    w_in_ab_bf = w_in_ab.astype(BF16)
    w_out_ab_bf = w_out_ab.astype(BF16)
    w_in_c_bf = w_in_c.astype(BF16)
    w_out_c_bf = w_out_c.astype(BF16)
    w_ff1_bf = w_ff1.astype(BF16)
    w_ff2_bf = w_ff2.astype(BF16)

    cab3 = conv_a_b.reshape(n_even, 1, D_A)
    lnag3 = ln_a_g.reshape(n_even, 1, D_A)
    lnab3 = ln_a_b.reshape(n_even, 1, D_A)
    binc3 = b_in_c.reshape(n_odd, 1, 2 * D_C)
    lnvg3 = ln_v_g.reshape(n_odd, 1, D_C)
    lnvb3 = ln_v_b.reshape(n_odd, 1, D_C)
    fg2 = final_g.reshape(1, D_MODEL)

    sbias = jnp.repeat(jnp.swapaxes(b_s, 1, 2), C_HEAD_DIM, axis=2)
    wm = jnp.repeat(jnp.transpose(w_s[:, :, :dec_seq, :dec_seq], (0, 2, 3, 1)),
                    C_HEAD_DIM, axis=3)

    mod = _ada_call(jnp.concatenate([c_sample, c_prompt], axis=0), w_ada, b_ada)

    xp = x_prompt
    xs = jnp.transpose(x_sample, (1, 0, 2))
    ha = jnp.transpose(state_conv_a, (0, 2, 1, 3))
    hb = jnp.transpose(state_conv_b, (0, 2, 1, 3))

    a_p, b_p, v_p, a_s, b_s_out, v_s = [], [], [], [], [], []
    for l in range(DEPTH):
        final = l == DEPTH - 1
        if l % 2 == 0:
            xp, sa, sb = _even_prompt_call(xp, mod, norm_g, w_in_ab_bf, conv_a_w, cab3, lnag3, lnab3,
                                           conv_b_w, w_out_ab_bf, l)
            a_p.append(sa)
            b_p.append(sb)
            xs, sa, sb = _even_sample_call(xs, mod, norm_g, ha, hb, w_in_ab_bf, conv_a_w, cab3, lnag3,
                                           lnab3, conv_b_w, w_out_ab_bf, l)
            a_s.append(jnp.transpose(sa, (1, 0, 2)))
            b_s_out.append(jnp.transpose(sb, (1, 0, 2)))
        else:
            xp, cv = _odd_prompt_call(xp, mod, norm_g, w_in_c_bf, binc3, lnvg3, lnvb3, w_s, sbias,
                                      w_out_c_bf, l)
            v_p.append(cv)
            xs, cv = _odd_sample_call(xs, mod, norm_g, w_in_c_bf, binc3, lnvg3, lnvb3, wm,
                                      sbias[:, :dec_seq], w_out_c_bf, l)
            v_s.append(jnp.transpose(cv, (1, 0, 2)))
        xp = _mlp_prompt_call(xp, mod, norm_g, w_ff1_bf, w_ff2_bf, fg2, l, final)
        xs = _mlp_sample_call(xs, mod, norm_g, w_ff1_bf, w_ff2_bf, fg2, l, final)

    return (xp, jnp.transpose(xs, (1, 0, 2)), jnp.stack(a_p), jnp.stack(a_s), jnp.stack(b_p),
            jnp.stack(b_s_out), jnp.stack(v_p), jnp.stack(v_s))
```

```python
import functools

import jax
import jax.numpy as jnp
from jax import lax
from jax.experimental import pallas as pl
from jax.experimental.pallas import tpu as pltpu

D_MODEL = 1024
DEPTH = 4
D_A = 512
D_B = 512
CONV_A = 31
CONV_B = 3
D_IN_EVEN = 2 * D_A + 3 * D_B
D_C = 1024
C_HEADS = 8
C_HEAD_DIM = 128
CHUNK = 128
D_FF = 4096
N_MOD = 6
EPS = 1e-6

BF16 = jnp.bfloat16
F32 = jnp.float32

VMEM_LIMIT_BYTES = 56 * 1024 * 1024
SUBLANES = 8
ROW_TILE = 512
SEQ_BLOCK = 32
FF_CHUNK = 1024
A_HIST_ROWS = 32
B_HIST_ROWS = 8
LANES = 128
CONV_SUB_ROWS = 256
CONV_ROW_BLOCK = 128


def _params():
    return pltpu.CompilerParams(
        dimension_semantics=("arbitrary", "arbitrary"),
        vmem_limit_bytes=VMEM_LIMIT_BYTES)


def _resident(shape, index_map):
    return pl.BlockSpec(shape, index_map, pipeline_mode=pl.Buffered(1))


def _rms(x, g):
    ms = jnp.mean(x * x, axis=-1, keepdims=True)
    return x * lax.rsqrt(ms + EPS) * g


def _rms_mod(x, g, shift, scale):
    ms = jnp.mean(x * x, axis=-1, keepdims=True)
    return x * lax.rsqrt(ms + EPS) * (g * (1.0 + scale)) + shift


def _layernorm(x, g, b):
    mu = jnp.mean(x, axis=-1, keepdims=True)
    xc = x - mu
    var = jnp.mean(xc * xc, axis=-1, keepdims=True)
    return xc * lax.rsqrt(var + EPS) * g + b


def _silu(x):
    return x * jax.nn.sigmoid(x)


def _dot(a, b):
    return jnp.dot(a, b, preferred_element_type=F32)


def _ada_kernel(c_ref, w_ref, b_ref, o_ref):
    c = c_ref[...]
    o_ref[...] = _dot(_silu(c).astype(BF16), w_ref[...].astype(BF16)) + b_ref[...]


def _ada_call(c_all, w_ada, b_ada):
    n = c_all.shape[0]
    b4 = b_ada.reshape(DEPTH, N_MOD, 1, D_MODEL)
    return pl.pallas_call(
        _ada_kernel,
        grid=(DEPTH, N_MOD),
        in_specs=[
            pl.BlockSpec((n, D_MODEL), lambda l, j: (0, 0)),
            pl.BlockSpec((None, D_MODEL, D_MODEL), lambda l, j: (l, 0, j)),
            pl.BlockSpec((None, None, 1, D_MODEL), lambda l, j: (l, j, 0, 0)),
        ],
        out_specs=pl.BlockSpec((None, None, n, D_MODEL), lambda l, j: (l, j, 0, 0)),
        out_shape=jax.ShapeDtypeStruct((DEPTH, N_MOD, n, D_MODEL), F32),
        compiler_params=_params(),
        name="adaln",
    )(c_all, w_ada, b4)


def _mlp_body(x2d, h_bf, gate, w1_ref, w2_ref):
    acc = None
    for j in range(D_FF // FF_CHUNK):
        cols = slice(j * FF_CHUNK, (j + 1) * FF_CHUNK)
        hid = _dot(h_bf, w1_ref[:, cols])
        hid = jnp.square(jnp.maximum(hid, 0.0)).astype(BF16)
        part = _dot(hid, w2_ref[cols, :])
        acc = part if acc is None else acc + part
    return x2d + gate * acc


def _mlp_prompt_kernel(x_ref, mod_ref, g_ref, w1_ref, w2_ref, fg_ref, o_ref, *, final):
    b = pl.program_id(0)
    x = x_ref[...]
    shift = mod_ref[3, pl.ds(b, 1), :]
    scale = mod_ref[4, pl.ds(b, 1), :]
    gate = mod_ref[5, pl.ds(b, 1), :]
    h = _rms_mod(x, g_ref[1:2, :], shift, scale).astype(BF16)
    y = _mlp_body(x, h, gate, w1_ref, w2_ref)
    if final:
        y = _rms(y, fg_ref[...])
    o_ref[...] = y


def _mlp_sample_kernel(x_ref, mod_ref, g_ref, w1_ref, w2_ref, fg_ref, o_ref, *, final):
    x = x_ref[...]
    t, s, d = x.shape
    h = _rms_mod(x, g_ref[1:2, :], mod_ref[3][None], mod_ref[4][None])
    h = h.reshape(t * s, d).astype(BF16)
    gate = jnp.broadcast_to(mod_ref[5][None], (t, s, d)).reshape(t * s, d)
    y = _mlp_body(x.reshape(t * s, d), h, gate, w1_ref, w2_ref)
    if final:
        y = _rms(y, fg_ref[...])
    o_ref[...] = y.reshape(t, s, d)


def _mlp_prompt_call(x, mod, norm_g, w1, w2, final_g, l, final):
    nb, seq, d = x.shape
    n_prompt_blk = (mod.shape[2] - nb) // nb
    return pl.pallas_call(
        functools.partial(_mlp_prompt_kernel, final=final),
        grid=(nb, seq // ROW_TILE),
        in_specs=[
            pl.BlockSpec((None, ROW_TILE, d), lambda b, t: (b, t, 0)),
            pl.BlockSpec((None, N_MOD, nb, d), lambda b, t: (l, 0, n_prompt_blk, 0)),
            pl.BlockSpec((None, 2, d), lambda b, t: (l, 0, 0)),
            _resident((None, d, D_FF), lambda b, t: (l, 0, 0)),
            _resident((None, D_FF, d), lambda b, t: (l, 0, 0)),
            pl.BlockSpec((1, d), lambda b, t: (0, 0)),
        ],
        out_specs=pl.BlockSpec((None, ROW_TILE, d), lambda b, t: (b, t, 0)),
        out_shape=jax.ShapeDtypeStruct(x.shape, F32),
        compiler_params=_params(),
        name=f"mlp_prompt_{l}",
    )(x, mod, norm_g, w1, w2, final_g)


def _mlp_sample_call(x, mod, norm_g, w1, w2, final_g, l, final):
    t, ns, d = x.shape
    return pl.pallas_call(
        functools.partial(_mlp_sample_kernel, final=final),
        grid=(ns // SEQ_BLOCK, 1),
        in_specs=[
            pl.BlockSpec((t, SEQ_BLOCK, d), lambda s, _: (0, s, 0)),
            pl.BlockSpec((None, N_MOD, SEQ_BLOCK, d), lambda s, _: (l, 0, s, 0)),
            pl.BlockSpec((None, 2, d), lambda s, _: (l, 0, 0)),
            _resident((None, d, D_FF), lambda s, _: (l, 0, 0)),
            _resident((None, D_FF, d), lambda s, _: (l, 0, 0)),
            pl.BlockSpec((1, d), lambda s, _: (0, 0)),
        ],
        out_specs=pl.BlockSpec((t, SEQ_BLOCK, d), lambda s, _: (0, s, 0)),
        out_shape=jax.ShapeDtypeStruct(x.shape, F32),
        compiler_params=_params(),
        name=f"mlp_sample_{l}",
    )(x, mod, norm_g, w1, w2, final_g)


def _even_prompt_kernel(x_ref, mod_ref, g_ref, w_in_ref, caw_ref, cab_ref, lng_ref, lnb_ref,
                        cbw_ref, w_out_ref, o_ref, sa_ref, sb_ref, abuf, bbuf, shbuf, cbuf):
    b = pl.program_id(0)
    t = pl.program_id(1)
    rows = x_ref.shape[0]
    sub = CONV_SUB_ROWS
    ext = sub + A_HIST_ROWS
    first_a = A_HIST_ROWS - (CONV_A - 1)
    first_b = B_HIST_ROWS - (CONV_B - 1)

    @pl.when(t == 0)
    def _():
        abuf[0:A_HIST_ROWS, :] = jnp.zeros((A_HIST_ROWS, D_A), F32)
        bbuf[0:B_HIST_ROWS, :] = jnp.zeros((B_HIST_ROWS, D_B), F32)

    shift = mod_ref[0, pl.ds(b, 1), :]
    scale = mod_ref[1, pl.ds(b, 1), :]
    gate = mod_ref[2, pl.ds(b, 1), :]

    for j in range(rows // sub):
        r0 = j * sub
        x = x_ref[r0:r0 + sub, :]
        h = _rms_mod(x, g_ref[0:1, :], shift, scale).astype(BF16)
        z = _dot(h, w_in_ref[...])

        abuf[A_HIST_ROWS + r0:A_HIST_ROWS + r0 + sub, :] = (
            z[:, 0:D_A] * jax.nn.sigmoid(z[:, D_A:2 * D_A]))
        a_ext = abuf[r0:r0 + ext, :]
        for s in range(1, SUBLANES):
            shbuf[j, s - 1] = pltpu.roll(a_ext, ext - s, axis=0)
        for c in range(D_A // LANES):
            lanes = slice(c * LANES, (c + 1) * LANES)
            for rb in range(sub // CONV_ROW_BLOCK):
                acc = jnp.broadcast_to(cab_ref[:, lanes], (CONV_ROW_BLOCK, LANES))
                for k in range(CONV_A):
                    q, s = divmod(first_a + k, SUBLANES)
                    start = q * SUBLANES + rb * CONV_ROW_BLOCK
                    if s == 0:
                        tap = abuf[r0 + start:r0 + start + CONV_ROW_BLOCK, lanes]
                    else:
                        tap = shbuf[j, s - 1, start:start + CONV_ROW_BLOCK, lanes]
                    acc = acc + caw_ref[k:k + 1, lanes] * tap
                cbuf[r0 + rb * CONV_ROW_BLOCK:r0 + (rb + 1) * CONV_ROW_BLOCK, lanes] = acc
        a_out = _silu(_layernorm(cbuf[r0:r0 + sub, :], lng_ref[...], lnb_ref[...]))

        bbuf[B_HIST_ROWS + r0:B_HIST_ROWS + r0 + sub, :] = z[:, 4 * D_A:5 * D_A] * z[:, 2 * D_A:3 * D_A]
        accb = cbw_ref[0:1, :] * bbuf[pl.ds(r0 + first_b, sub), :]
        for k in range(1, CONV_B):
            accb = accb + cbw_ref[k:k + 1, :] * bbuf[pl.ds(r0 + first_b + k, sub), :]
        b_out = z[:, 3 * D_A:4 * D_A] * accb

        y = _dot(jnp.concatenate([a_out, b_out], axis=-1).astype(BF16), w_out_ref[...])
        o_ref[r0:r0 + sub, :] = x + gate * y

    @pl.when(t == pl.num_programs(1) - 1)
    def _():
        sa_ref[...] = abuf[pl.ds(rows + first_a, CONV_A - 1), :]
        sb_ref[...] = bbuf[pl.ds(rows + first_b, CONV_B - 1), :]

    abuf[0:A_HIST_ROWS, :] = abuf[rows:rows + A_HIST_ROWS, :]
    bbuf[0:B_HIST_ROWS, :] = bbuf[rows:rows + B_HIST_ROWS, :]


def _even_prompt_call(x, mod, norm_g, w_in, caw, cab, lng, lnb, cbw, w_out, l):
    nb, seq, d = x.shape
    e = l // 2
    n_prompt_blk = (mod.shape[2] - nb) // nb
    vec = lambda width: pl.BlockSpec((None, 1, width), lambda b, t: (e, 0, 0))
    return pl.pallas_call(
        _even_prompt_kernel,
        grid=(nb, seq // ROW_TILE),
        in_specs=[
            pl.BlockSpec((None, ROW_TILE, d), lambda b, t: (b, t, 0)),
            pl.BlockSpec((None, N_MOD, nb, d), lambda b, t: (l, 0, n_prompt_blk, 0)),
            pl.BlockSpec((None, 2, d), lambda b, t: (l, 0, 0)),
            _resident((None, d, D_IN_EVEN), lambda b, t: (e, 0, 0)),
            pl.BlockSpec((None, CONV_A, D_A), lambda b, t: (e, 0, 0)),
            vec(D_A), vec(D_A), vec(D_A),
            pl.BlockSpec((None, CONV_B, D_B), lambda b, t: (e, 0, 0)),
            _resident((None, D_A + D_B, d), lambda b, t: (e, 0, 0)),
        ],
        out_specs=[
            pl.BlockSpec((None, ROW_TILE, d), lambda b, t: (b, t, 0)),
            pl.BlockSpec((None, CONV_A - 1, D_A), lambda b, t: (b, 0, 0)),
            pl.BlockSpec((None, CONV_B - 1, D_B), lambda b, t: (b, 0, 0)),
        ],
        out_shape=[
            jax.ShapeDtypeStruct(x.shape, F32),
            jax.ShapeDtypeStruct((nb, CONV_A - 1, D_A), F32),
            jax.ShapeDtypeStruct((nb, CONV_B - 1, D_B), F32),
        ],
        scratch_shapes=[
            pltpu.VMEM((ROW_TILE + A_HIST_ROWS, D_A), F32),
            pltpu.VMEM((ROW_TILE + B_HIST_ROWS, D_B), F32),
            pltpu.VMEM((ROW_TILE // CONV_SUB_ROWS, SUBLANES - 1, CONV_SUB_ROWS + A_HIST_ROWS, D_A), F32),
            pltpu.VMEM((ROW_TILE, D_A), F32),
        ],
        compiler_params=_params(),
        name=f"even_prompt_{l}",
    )(x, mod, norm_g, w_in, caw, cab, lng, lnb, cbw, w_out)


def _even_sample_kernel(x_ref, mod_ref, g_ref, ha_ref, hb_ref, w_in_ref, caw_ref, cab_ref,
                        lng_ref, lnb_ref, cbw_ref, w_out_ref, o_ref, sa_ref, sb_ref):
    x = x_ref[...]
    t, s, d = x.shape
    h = _rms_mod(x, g_ref[0:1, :], mod_ref[0][None], mod_ref[1][None])
    z = _dot(h.reshape(t * s, d).astype(BF16), w_in_ref[...]).reshape(t, s, D_IN_EVEN)

    a = z[:, :, 0:D_A] * jax.nn.sigmoid(z[:, :, D_A:2 * D_A])
    a_ext = jnp.concatenate([ha_ref[...], a], axis=0)
    acc = jnp.broadcast_to(cab_ref[...][None], (t, s, D_A))
    for k in range(CONV_A):
        acc = acc + caw_ref[k:k + 1, :][None] * a_ext[k:k + t]
    a_out = _silu(_layernorm(acc, lng_ref[...][None], lnb_ref[...][None]))
    sa_ref[...] = a_ext[t:]

    bx = z[:, :, 4 * D_A:5 * D_A] * z[:, :, 2 * D_A:3 * D_A]
    b_ext = jnp.concatenate([hb_ref[...], bx], axis=0)
    accb = cbw_ref[0:1, :][None] * b_ext[0:t]
    for k in range(1, CONV_B):
        accb = accb + cbw_ref[k:k + 1, :][None] * b_ext[k:k + t]
    b_out = z[:, :, 3 * D_A:4 * D_A] * accb
    sb_ref[...] = b_ext[t:]

    cat = jnp.concatenate([a_out, b_out], axis=-1).reshape(t * s, D_A + D_B).astype(BF16)
    y = _dot(cat, w_out_ref[...]).reshape(t, s, d)
    o_ref[...] = x + mod_ref[2][None] * y


def _even_sample_call(x, mod, norm_g, hist_a, hist_b, w_in, caw, cab, lng, lnb, cbw, w_out, l):
    t, ns, d = x.shape
    e = l // 2
    vec = lambda width: pl.BlockSpec((None, 1, width), lambda s, _: (e, 0, 0))
    return pl.pallas_call(
        _even_sample_kernel,
        grid=(ns // SEQ_BLOCK, 1),
        in_specs=[
            pl.BlockSpec((t, SEQ_BLOCK, d), lambda s, _: (0, s, 0)),
            pl.BlockSpec((None, N_MOD, SEQ_BLOCK, d), lambda s, _: (l, 0, s, 0)),
            pl.BlockSpec((None, 2, d), lambda s, _: (l, 0, 0)),
            pl.BlockSpec((None, CONV_A - 1, SEQ_BLOCK, D_A), lambda s, _: (e, 0, s, 0)),
            pl.BlockSpec((None, CONV_B - 1, SEQ_BLOCK, D_B), lambda s, _: (e, 0, s, 0)),
            _resident((None, d, D_IN_EVEN), lambda s, _: (e, 0, 0)),
            pl.BlockSpec((None, CONV_A, D_A), lambda s, _: (e, 0, 0)),
            vec(D_A), vec(D_A), vec(D_A),
            pl.BlockSpec((None, CONV_B, D_B), lambda s, _: (e, 0, 0)),
            _resident((None, D_A + D_B, d), lambda s, _: (e, 0, 0)),
        ],
        out_specs=[
            pl.BlockSpec((t, SEQ_BLOCK, d), lambda s, _: (0, s, 0)),
            pl.BlockSpec((CONV_A - 1, SEQ_BLOCK, D_A), lambda s, _: (0, s, 0)),
            pl.BlockSpec((CONV_B - 1, SEQ_BLOCK, D_B), lambda s, _: (0, s, 0)),
        ],
        out_shape=[
            jax.ShapeDtypeStruct(x.shape, F32),
            jax.ShapeDtypeStruct((CONV_A - 1, ns, D_A), F32),
            jax.ShapeDtypeStruct((CONV_B - 1, ns, D_B), F32),
        ],
        compiler_params=_params(),
        name=f"even_sample_{l}",
    )(x, mod, norm_g, hist_a, hist_b, w_in, caw, cab, lng, lnb, cbw, w_out)


def _odd_prompt_kernel(x_ref, mod_ref, g_ref, w_in_ref, b_in_ref, lng_ref, lnb_ref, ws_ref,
                       sbias_ref, w_out_ref, o_ref, cv_ref):
    b = pl.program_id(0)
    t = pl.program_id(1)
    rows = x_ref.shape[0]
    x = x_ref[...]
    shift = mod_ref[0, pl.ds(b, 1), :]
    scale = mod_ref[1, pl.ds(b, 1), :]
    gate = mod_ref[2, pl.ds(b, 1), :]
    h = _rms_mod(x, g_ref[0:1, :], shift, scale).astype(BF16)
    z = jax.nn.gelu(_dot(h, w_in_ref[...]) + b_in_ref[...])
    u = z[:, 0:D_C]
    v = _layernorm(z[:, D_C:2 * D_C], lng_ref[...], lnb_ref[...])

    @pl.when(t == pl.num_programs(1) - 1)
    def _():
        cv_ref[...] = v[rows - CHUNK:rows, :]

    v_bf = v.astype(BF16)
    causal = (lax.broadcasted_iota(jnp.int32, (CHUNK, CHUNK), 0)
              >= lax.broadcasted_iota(jnp.int32, (CHUNK, CHUNK), 1))
    ws = [jnp.where(causal, ws_ref[hd], 0.0).astype(BF16) for hd in range(C_HEADS)]
    s_rows = []
    for c in range(rows // CHUNK):
        heads = [
            _dot(ws[hd], v_bf[c * CHUNK:(c + 1) * CHUNK, hd * C_HEAD_DIM:(hd + 1) * C_HEAD_DIM])
            for hd in range(C_HEADS)
        ]
        s_rows.append(jnp.concatenate(heads, axis=-1) + sbias_ref[...])
    s = jnp.concatenate(s_rows, axis=0)
    y = _dot((u * s).astype(BF16), w_out_ref[...])
    o_ref[...] = x + gate * y


def _odd_prompt_call(x, mod, norm_g, w_in, b_in, lng, lnb, w_s, sbias, w_out, l):
    nb, seq, d = x.shape
    o = l // 2
    n_prompt_blk = (mod.shape[2] - nb) // nb
    return pl.pallas_call(
        _odd_prompt_kernel,
        grid=(nb, seq // ROW_TILE),
        in_specs=[
            pl.BlockSpec((None, ROW_TILE, d), lambda b, t: (b, t, 0)),
            pl.BlockSpec((None, N_MOD, nb, d), lambda b, t: (l, 0, n_prompt_blk, 0)),
            pl.BlockSpec((None, 2, d), lambda b, t: (l, 0, 0)),
            _resident((None, d, 2 * D_C), lambda b, t: (o, 0, 0)),
            pl.BlockSpec((None, 1, 2 * D_C), lambda b, t: (o, 0, 0)),
            pl.BlockSpec((None, 1, D_C), lambda b, t: (o, 0, 0)),
            pl.BlockSpec((None, 1, D_C), lambda b, t: (o, 0, 0)),
            pl.BlockSpec((None, C_HEADS, CHUNK, CHUNK), lambda b, t: (o, 0, 0, 0)),
            pl.BlockSpec((None, CHUNK, D_C), lambda b, t: (o, 0, 0)),
            _resident((None, D_C, d), lambda b, t: (o, 0, 0)),
        ],
        out_specs=[
            pl.BlockSpec((None, ROW_TILE, d), lambda b, t: (b, t, 0)),
            pl.BlockSpec((None, CHUNK, D_C), lambda b, t: (b, 0, 0)),
        ],
        out_shape=[
            jax.ShapeDtypeStruct(x.shape, F32),
            jax.ShapeDtypeStruct((nb, CHUNK, D_C), F32),
        ],
        compiler_params=_params(),
        name=f"odd_prompt_{l}",
    )(x, mod, norm_g, w_in, b_in, lng, lnb, w_s, sbias, w_out)


def _odd_sample_kernel(x_ref, mod_ref, g_ref, w_in_ref, b_in_ref, lng_ref, lnb_ref, wm_ref,
                       sbias_ref, w_out_ref, o_ref, cv_ref):
    x = x_ref[...]
    t, s, d = x.shape
    h = _rms_mod(x, g_ref[0:1, :], mod_ref[0][None], mod_ref[1][None])
    z = jax.nn.gelu(_dot(h.reshape(t * s, d).astype(BF16), w_in_ref[...]) + b_in_ref[...])
    z = z.reshape(t, s, 2 * D_C)
    u = z[:, :, 0:D_C]
    v = _layernorm(z[:, :, D_C:2 * D_C], lng_ref[...][None], lnb_ref[...][None])
    cv_ref[...] = v
    gated = []
    for i in range(t):
        s_i = jnp.broadcast_to(sbias_ref[i:i + 1, :], (s, D_C))
        for j in range(i + 1):
            s_i = s_i + wm_ref[i, j:j + 1, :] * v[j]
        gated.append(u[i] * s_i)
    us = jnp.stack(gated, axis=0).reshape(t * s, D_C).astype(BF16)
    y = _dot(us, w_out_ref[...]).reshape(t, s, d)
    o_ref[...] = x + mod_ref[2][None] * y


def _odd_sample_call(x, mod, norm_g, w_in, b_in, lng, lnb, wm, sbias, w_out, l):
    t, ns, d = x.shape
    o = l // 2
    return pl.pallas_call(
        _odd_sample_kernel,
        grid=(ns // SEQ_BLOCK, 1),
        in_specs=[
            pl.BlockSpec((t, SEQ_BLOCK, d), lambda s, _: (0, s, 0)),
            pl.BlockSpec((None, N_MOD, SEQ_BLOCK, d), lambda s, _: (l, 0, s, 0)),
            pl.BlockSpec((None, 2, d), lambda s, _: (l, 0, 0)),
            _resident((None, d, 2 * D_C), lambda s, _: (o, 0, 0)),
            pl.BlockSpec((None, 1, 2 * D_C), lambda s, _: (o, 0, 0)),
            pl.BlockSpec((None, 1, D_C), lambda s, _: (o, 0, 0)),
            pl.BlockSpec((None, 1, D_C), lambda s, _: (o, 0, 0)),
            pl.BlockSpec((None, t, t, D_C), lambda s, _: (o, 0, 0, 0)),
            pl.BlockSpec((None, t, D_C), lambda s, _: (o, 0, 0)),
            _resident((None, D_C, d), lambda s, _: (o, 0, 0)),
        ],
        out_specs=[
            pl.BlockSpec((t, SEQ_BLOCK, d), lambda s, _: (0, s, 0)),
            pl.BlockSpec((t, SEQ_BLOCK, D_C), lambda s, _: (0, s, 0)),
        ],
        out_shape=[
            jax.ShapeDtypeStruct(x.shape, F32),
            jax.ShapeDtypeStruct((t, ns, D_C), F32),
        ],
        compiler_params=_params(),
        name=f"odd_sample_{l}",
    )(x, mod, norm_g, w_in, b_in, lng, lnb, wm, sbias, w_out)


def kernel(x_prompt, x_sample, state_conv_a, state_conv_b, c_prompt, c_sample, w_in_ab, conv_a_w, conv_a_b, ln_a_g, ln_a_b, conv_b_w, w_out_ab, w_in_c, b_in_c, ln_v_g, ln_v_b, w_s, b_s, w_out_c, w_ada, b_ada, norm_g, w_ff1, w_ff2, final_g):
    dec_seq = x_sample.shape[1]
    n_even, n_odd = w_in_ab.shape[0], w_in_c.shape[0]

    w_in_ab_bf = w_in_ab.astype(BF16)
    w_out_ab_bf = w_out_ab.astype(BF16)
    w_in_c_bf = w_in_c.astype(BF16)
    w_out_c_bf = w_out_c.astype(BF16)
    w_ff1_bf = w_ff1.astype(BF16)
    w_ff2_bf = w_ff2.astype(BF16)

    cab3 = conv_a_b.reshape(n_even, 1, D_A)
    lnag3 = ln_a_g.reshape(n_even, 1, D_A)
    lnab3 = ln_a_b.reshape(n_even, 1, D_A)
    binc3 = b_in_c.reshape(n_odd, 1, 2 * D_C)
    lnvg3 = ln_v_g.reshape(n_odd, 1, D_C)
    lnvb3 = ln_v_b.reshape(n_odd, 1, D_C)
    fg2 = final_g.reshape(1, D_MODEL)

    sbias = jnp.repeat(jnp.swapaxes(b_s, 1, 2), C_HEAD_DIM, axis=2)
    wm = jnp.repeat(jnp.transpose(w_s[:, :, :dec_seq, :dec_seq], (0, 2, 3, 1)),
                    C_HEAD_DIM, axis=3)

    mod = _ada_call(jnp.concatenate([c_sample, c_prompt], axis=0), w_ada, b_ada)

    xp = x_prompt
    xs = jnp.transpose(x_sample, (1, 0, 2))
    ha = jnp.transpose(state_conv_a, (0, 2, 1, 3))
    hb = jnp.transpose(state_conv_b, (0, 2, 1, 3))

    a_p, b_p, v_p, a_s, b_s_out, v_s = [], [], [], [], [], []
    for l in range(DEPTH):
        final = l == DEPTH - 1
        if l % 2 == 0:
            xp, sa, sb = _even_prompt_call(xp, mod, norm_g, w_in_ab_bf, conv_a_w, cab3, lnag3, lnab3,
                                           conv_b_w, w_out_ab_bf, l)
            a_p.append(sa)
            b_p.append(sb)
            xs, sa, sb = _even_sample_call(xs, mod, norm_g, ha, hb, w_in_ab_bf, conv_a_w, cab3, lnag3,
                                           lnab3, conv_b_w, w_out_ab_bf, l)
            a_s.append(jnp.transpose(sa, (1, 0, 2)))
            b_s_out.append(jnp.transpose(sb, (1, 0, 2)))
        else:
            xp, cv = _odd_prompt_call(xp, mod, norm_g, w_in_c_bf, binc3, lnvg3, lnvb3, w_s, sbias,
                                      w_out_c_bf, l)
            v_p.append(cv)
            xs, cv = _odd_sample_call(xs, mod, norm_g, w_in_c_bf, binc3, lnvg3, lnvb3, wm,
                                      sbias[:, :dec_seq], w_out_c_bf, l)
            v_s.append(jnp.transpose(cv, (1, 0, 2)))
        xp = _mlp_prompt_call(xp, mod, norm_g, w_ff1_bf, w_ff2_bf, fg2, l, final)
        xs = _mlp_sample_call(xs, mod, norm_g, w_ff1_bf, w_ff2_bf, fg2, l, final)

    return (xp, jnp.transpose(xs, (1, 0, 2)), jnp.stack(a_p), jnp.stack(a_s), jnp.stack(b_p),
            jnp.stack(b_s_out), jnp.stack(v_p), jnp.stack(v_s))
```

```python
import functools

import jax
import jax.numpy as jnp
from jax import lax
from jax.experimental import pallas as pl
from jax.experimental.pallas import tpu as pltpu

D_MODEL = 1024
DEPTH = 4
D_A = 512
D_B = 512
CONV_A = 31
CONV_B = 3
D_IN_EVEN = 2 * D_A + 3 * D_B
D_C = 1024
C_HEADS = 8
C_HEAD_DIM = 128
CHUNK = 128
D_FF = 4096
N_MOD = 6
EPS = 1e-6

BF16 = jnp.bfloat16
F32 = jnp.float32

VMEM_LIMIT_BYTES = 56 * 1024 * 1024
SUBLANES = 8
ROW_TILE = 512
MLP_ROW_TILE = 1024
SEQ_BLOCK = 64
FF_CHUNK = 1024
A_HIST_ROWS = 32
B_HIST_ROWS = 8
LANES = 128
CONV_SUB_ROWS = 256
CONV_ROW_BLOCK = 128


def _params():
    return pltpu.CompilerParams(
        dimension_semantics=("arbitrary", "arbitrary"),
        vmem_limit_bytes=VMEM_LIMIT_BYTES)


def _resident(shape, index_map):
    return pl.BlockSpec(shape, index_map, pipeline_mode=pl.Buffered(1))


def _with_weight_casts(body, n_in, n_out, n_cast):
    def wrapped(*refs):
        ins = refs[:n_in]
        cast_src = refs[n_in:n_in + n_cast]
        outs = refs[n_in + n_cast:n_in + n_cast + n_out]
        cast_dst = refs[n_in + n_cast + n_out:n_in + 2 * n_cast + n_out]
        scratch = refs[n_in + 2 * n_cast + n_out:]
        for src, dst in zip(cast_src, cast_dst):
            dst[...] = src[...].astype(BF16)
        body(*ins, *outs, *scratch)
    return wrapped


def _cast_specs(casts, n_outer, n_inner):
    in_specs, out_specs, out_shapes, args = [], [], [], []
    for w, layer in casts:
        _, r, c = w.shape
        rb = r // (n_outer * n_inner)
        in_specs.append(pl.BlockSpec((None, rb, c), lambda b, t, layer=layer: (layer, b * n_inner + t, 0)))
        out_specs.append(pl.BlockSpec((rb, c), lambda b, t: (b * n_inner + t, 0)))
        out_shapes.append(jax.ShapeDtypeStruct((r, c), BF16))
        args.append(w)
    return in_specs, out_specs, out_shapes, args


def _rms(x, g):
    ms = jnp.mean(x * x, axis=-1, keepdims=True)
    return x * lax.rsqrt(ms + EPS) * g


def _rms_mod(x, g, shift, scale):
    ms = jnp.mean(x * x, axis=-1, keepdims=True)
    return x * lax.rsqrt(ms + EPS) * (g * (1.0 + scale)) + shift


def _layernorm(x, g, b):
    mu = jnp.mean(x, axis=-1, keepdims=True)
    xc = x - mu
    var = jnp.mean(xc * xc, axis=-1, keepdims=True)
    return xc * lax.rsqrt(var + EPS) * g + b


def _silu(x):
    return x * jax.nn.sigmoid(x)


def _dot(a, b):
    return jnp.dot(a, b, preferred_element_type=F32)


def _ada_kernel(c_ref, w_ref, b_ref, o_ref):
    c = c_ref[...]
    o_ref[...] = _dot(_silu(c).astype(BF16), w_ref[...].astype(BF16)) + b_ref[...]


def _ada_call(c_all, w_ada, b_ada):
    n = c_all.shape[0]
    b4 = b_ada.reshape(DEPTH, N_MOD, 1, D_MODEL)
    return pl.pallas_call(
        _ada_kernel,
        grid=(DEPTH, N_MOD),
        in_specs=[
            pl.BlockSpec((n, D_MODEL), lambda l, j: (0, 0)),
            pl.BlockSpec((None, D_MODEL, D_MODEL), lambda l, j: (l, 0, j)),
            pl.BlockSpec((None, None, 1, D_MODEL), lambda l, j: (l, j, 0, 0)),
        ],
        out_specs=pl.BlockSpec((None, None, n, D_MODEL), lambda l, j: (l, j, 0, 0)),
        out_shape=jax.ShapeDtypeStruct((DEPTH, N_MOD, n, D_MODEL), F32),
        compiler_params=_params(),
        name="adaln",
    )(c_all, w_ada, b4)


def _mlp_body(x2d, h_bf, gate, w1_ref, w2_ref):
    acc = None
    for j in range(D_FF // FF_CHUNK):
        cols = slice(j * FF_CHUNK, (j + 1) * FF_CHUNK)
        hid = _dot(h_bf, w1_ref[:, cols])
        hid = jnp.square(jnp.maximum(hid, 0.0)).astype(BF16)
        part = _dot(hid, w2_ref[cols, :])
        acc = part if acc is None else acc + part
    return x2d + gate * acc


def _mlp_prompt_kernel(x_ref, mod_ref, g_ref, w1_ref, w2_ref, fg_ref, o_ref, *, final):
    b = pl.program_id(0)
    x = x_ref[...]
    shift = mod_ref[3, pl.ds(b, 1), :]
    scale = mod_ref[4, pl.ds(b, 1), :]
    gate = mod_ref[5, pl.ds(b, 1), :]
    h = _rms_mod(x, g_ref[1:2, :], shift, scale).astype(BF16)
    y = _mlp_body(x, h, gate, w1_ref, w2_ref)
    if final:
        y = _rms(y, fg_ref[...])
    o_ref[...] = y


def _mlp_sample_kernel(x_ref, mod_ref, g_ref, w1_ref, w2_ref, fg_ref, o_ref, *, final):
    x = x_ref[...]
    t, s, d = x.shape
    h = _rms_mod(x, g_ref[1:2, :], mod_ref[3][None], mod_ref[4][None])
    h = h.reshape(t * s, d).astype(BF16)
    gate = jnp.broadcast_to(mod_ref[5][None], (t, s, d)).reshape(t * s, d)
    y = _mlp_body(x.reshape(t * s, d), h, gate, w1_ref, w2_ref)
    if final:
        y = _rms(y, fg_ref[...])
    o_ref[...] = y.reshape(t, s, d)


def _mlp_prompt_call(x, mod, norm_g, w1, w2, final_g, l, final, casts):
    nb, seq, d = x.shape
    n_prompt_blk = (mod.shape[2] - nb) // nb
    n_tiles = seq // MLP_ROW_TILE
    c_in, c_out, c_shapes, c_args = _cast_specs(casts, nb, n_tiles)
    outs = pl.pallas_call(
        _with_weight_casts(functools.partial(_mlp_prompt_kernel, final=final), 6, 1, len(casts)),
        grid=(nb, n_tiles),
        in_specs=[
            pl.BlockSpec((None, MLP_ROW_TILE, d), lambda b, t: (b, t, 0)),
            pl.BlockSpec((None, N_MOD, nb, d), lambda b, t: (l, 0, n_prompt_blk, 0)),
            pl.BlockSpec((None, 2, d), lambda b, t: (l, 0, 0)),
            _resident((d, D_FF), lambda b, t: (0, 0)),
            _resident((D_FF, d), lambda b, t: (0, 0)),
            pl.BlockSpec((1, d), lambda b, t: (0, 0)),
        ] + c_in,
        out_specs=[pl.BlockSpec((None, MLP_ROW_TILE, d), lambda b, t: (b, t, 0))] + c_out,
        out_shape=[jax.ShapeDtypeStruct(x.shape, F32)] + c_shapes,
        compiler_params=_params(),
        name=f"mlp_prompt_{l}",
    )(x, mod, norm_g, w1, w2, final_g, *c_args)
    return outs[0], outs[1:]


def _mlp_sample_call(x, mod, norm_g, w1, w2, final_g, l, final):
    t, ns, d = x.shape
    return pl.pallas_call(
        functools.partial(_mlp_sample_kernel, final=final),
        grid=(ns // SEQ_BLOCK, 1),
        in_specs=[
            pl.BlockSpec((t, SEQ_BLOCK, d), lambda s, _: (0, s, 0)),
            pl.BlockSpec((None, N_MOD, SEQ_BLOCK, d), lambda s, _: (l, 0, s, 0)),
            pl.BlockSpec((None, 2, d), lambda s, _: (l, 0, 0)),
            _resident((d, D_FF), lambda s, _: (0, 0)),
            _resident((D_FF, d), lambda s, _: (0, 0)),
            pl.BlockSpec((1, d), lambda s, _: (0, 0)),
        ],
        out_specs=pl.BlockSpec((t, SEQ_BLOCK, d), lambda s, _: (0, s, 0)),
        out_shape=jax.ShapeDtypeStruct(x.shape, F32),
        compiler_params=_params(),
        name=f"mlp_sample_{l}",
    )(x, mod, norm_g, w1, w2, final_g)


def _even_prompt_kernel(x_ref, mod_ref, g_ref, w_in_ref, caw_ref, cab_ref, lng_ref, lnb_ref,
                        cbw_ref, w_out_ref, o_ref, sa_ref, sb_ref, abuf, bbuf, shbuf, cbuf):
    b = pl.program_id(0)
    t = pl.program_id(1)
    rows = x_ref.shape[0]
    sub = CONV_SUB_ROWS
    ext = sub + A_HIST_ROWS
    first_a = A_HIST_ROWS - (CONV_A - 1)
    first_b = B_HIST_ROWS - (CONV_B - 1)

    @pl.when(t == 0)
    def _():
        abuf[0:A_HIST_ROWS, :] = jnp.zeros((A_HIST_ROWS, D_A), F32)
        bbuf[0:B_HIST_ROWS, :] = jnp.zeros((B_HIST_ROWS, D_B), F32)

    shift = mod_ref[0, pl.ds(b, 1), :]
    scale = mod_ref[1, pl.ds(b, 1), :]
    gate = mod_ref[2, pl.ds(b, 1), :]

    for j in range(rows // sub):
        r0 = j * sub
        x = x_ref[r0:r0 + sub, :]
        h = _rms_mod(x, g_ref[0:1, :], shift, scale).astype(BF16)
        z = _dot(h, w_in_ref[...])

        abuf[A_HIST_ROWS + r0:A_HIST_ROWS + r0 + sub, :] = (
            z[:, 0:D_A] * jax.nn.sigmoid(z[:, D_A:2 * D_A]))
        a_ext = abuf[r0:r0 + ext, :]
        for s in range(1, SUBLANES):
            shbuf[j, s - 1] = pltpu.roll(a_ext, ext - s, axis=0)
        for c in range(D_A // LANES):
            lanes = slice(c * LANES, (c + 1) * LANES)
            for rb in range(sub // CONV_ROW_BLOCK):
                acc = jnp.broadcast_to(cab_ref[:, lanes], (CONV_ROW_BLOCK, LANES))
                for k in range(CONV_A):
                    q, s = divmod(first_a + k, SUBLANES)
                    start = q * SUBLANES + rb * CONV_ROW_BLOCK
                    if s == 0:
                        tap = abuf[r0 + start:r0 + start + CONV_ROW_BLOCK, lanes]
                    else:
                        tap = shbuf[j, s - 1, start:start + CONV_ROW_BLOCK, lanes]
                    acc = acc + caw_ref[k:k + 1, lanes] * tap
                cbuf[r0 + rb * CONV_ROW_BLOCK:r0 + (rb + 1) * CONV_ROW_BLOCK, lanes] = acc
        a_out = _silu(_layernorm(cbuf[r0:r0 + sub, :], lng_ref[...], lnb_ref[...]))

        bbuf[B_HIST_ROWS + r0:B_HIST_ROWS + r0 + sub, :] = z[:, 4 * D_A:5 * D_A] * z[:, 2 * D_A:3 * D_A]
        accb = cbw_ref[0:1, :] * bbuf[pl.ds(r0 + first_b, sub), :]
        for k in range(1, CONV_B):
            accb = accb + cbw_ref[k:k + 1, :] * bbuf[pl.ds(r0 + first_b + k, sub), :]
        b_out = z[:, 3 * D_A:4 * D_A] * accb

        y = _dot(jnp.concatenate([a_out, b_out], axis=-1).astype(BF16), w_out_ref[...])
        o_ref[r0:r0 + sub, :] = x + gate * y

    @pl.when(t == pl.num_programs(1) - 1)
    def _():
        sa_ref[...] = abuf[pl.ds(rows + first_a, CONV_A - 1), :]
        sb_ref[...] = bbuf[pl.ds(rows + first_b, CONV_B - 1), :]

    abuf[0:A_HIST_ROWS, :] = abuf[rows:rows + A_HIST_ROWS, :]
    bbuf[0:B_HIST_ROWS, :] = bbuf[rows:rows + B_HIST_ROWS, :]


def _even_prompt_call(x, mod, norm_g, w_in, caw, cab, lng, lnb, cbw, w_out, l, casts):
    nb, seq, d = x.shape
    e = l // 2
    n_prompt_blk = (mod.shape[2] - nb) // nb
    n_tiles = seq // ROW_TILE
    c_in, c_out, c_shapes, c_args = _cast_specs(casts, nb, n_tiles)
    vec = lambda width: pl.BlockSpec((None, 1, width), lambda b, t: (e, 0, 0))
    outs = pl.pallas_call(
        _with_weight_casts(_even_prompt_kernel, 10, 3, len(casts)),
        grid=(nb, n_tiles),
        in_specs=[
            pl.BlockSpec((None, ROW_TILE, d), lambda b, t: (b, t, 0)),
            pl.BlockSpec((None, N_MOD, nb, d), lambda b, t: (l, 0, n_prompt_blk, 0)),
            pl.BlockSpec((None, 2, d), lambda b, t: (l, 0, 0)),
            _resident((d, D_IN_EVEN), lambda b, t: (0, 0)),
            pl.BlockSpec((None, CONV_A, D_A), lambda b, t: (e, 0, 0)),
            vec(D_A), vec(D_A), vec(D_A),
            pl.BlockSpec((None, CONV_B, D_B), lambda b, t: (e, 0, 0)),
            _resident((D_A + D_B, d), lambda b, t: (0, 0)),
        ] + c_in,
        out_specs=[
            pl.BlockSpec((None, ROW_TILE, d), lambda b, t: (b, t, 0)),
            pl.BlockSpec((None, CONV_A - 1, D_A), lambda b, t: (b, 0, 0)),
            pl.BlockSpec((None, CONV_B - 1, D_B), lambda b, t: (b, 0, 0)),
        ] + c_out,
        out_shape=[
            jax.ShapeDtypeStruct(x.shape, F32),
            jax.ShapeDtypeStruct((nb, CONV_A - 1, D_A), F32),
            jax.ShapeDtypeStruct((nb, CONV_B - 1, D_B), F32),
        ] + c_shapes,
        scratch_shapes=[
            pltpu.VMEM((ROW_TILE + A_HIST_ROWS, D_A), F32),
            pltpu.VMEM((ROW_TILE + B_HIST_ROWS, D_B), F32),
            pltpu.VMEM((ROW_TILE // CONV_SUB_ROWS, SUBLANES - 1, CONV_SUB_ROWS + A_HIST_ROWS, D_A), F32),
            pltpu.VMEM((ROW_TILE, D_A), F32),
        ],
        compiler_params=_params(),
        name=f"even_prompt_{l}",
    )(x, mod, norm_g, w_in, caw, cab, lng, lnb, cbw, w_out, *c_args)
    return outs[0], outs[1], outs[2], outs[3:]


def _even_sample_kernel(x_ref, mod_ref, g_ref, ha_ref, hb_ref, w_in_ref, caw_ref, cab_ref,
                        lng_ref, lnb_ref, cbw_ref, w_out_ref, o_ref, sa_ref, sb_ref):
    x = x_ref[...]
    t, s, d = x.shape
    h = _rms_mod(x, g_ref[0:1, :], mod_ref[0][None], mod_ref[1][None])
    z = _dot(h.reshape(t * s, d).astype(BF16), w_in_ref[...]).reshape(t, s, D_IN_EVEN)

    a = z[:, :, 0:D_A] * jax.nn.sigmoid(z[:, :, D_A:2 * D_A])
    a_ext = jnp.concatenate([ha_ref[...], a], axis=0)
    acc = jnp.broadcast_to(cab_ref[...][None], (t, s, D_A))
    for k in range(CONV_A):
        acc = acc + caw_ref[k:k + 1, :][None] * a_ext[k:k + t]
    a_out = _silu(_layernorm(acc, lng_ref[...][None], lnb_ref[...][None]))
    sa_ref[...] = a_ext[t:]

    bx = z[:, :, 4 * D_A:5 * D_A] * z[:, :, 2 * D_A:3 * D_A]
    b_ext = jnp.concatenate([hb_ref[...], bx], axis=0)
    accb = cbw_ref[0:1, :][None] * b_ext[0:t]
    for k in range(1, CONV_B):
        accb = accb + cbw_ref[k:k + 1, :][None] * b_ext[k:k + t]
    b_out = z[:, :, 3 * D_A:4 * D_A] * accb
    sb_ref[...] = b_ext[t:]

    cat = jnp.concatenate([a_out, b_out], axis=-1).reshape(t * s, D_A + D_B).astype(BF16)
    y = _dot(cat, w_out_ref[...]).reshape(t, s, d)
    o_ref[...] = x + mod_ref[2][None] * y


def _even_sample_call(x, mod, norm_g, hist_a, hist_b, w_in, caw, cab, lng, lnb, cbw, w_out, l):
    t, ns, d = x.shape
    e = l // 2
    vec = lambda width: pl.BlockSpec((None, 1, width), lambda s, _: (e, 0, 0))
    return pl.pallas_call(
        _even_sample_kernel,
        grid=(ns // SEQ_BLOCK, 1),
        in_specs=[
            pl.BlockSpec((t, SEQ_BLOCK, d), lambda s, _: (0, s, 0)),
            pl.BlockSpec((None, N_MOD, SEQ_BLOCK, d), lambda s, _: (l, 0, s, 0)),
            pl.BlockSpec((None, 2, d), lambda s, _: (l, 0, 0)),
            pl.BlockSpec((None, CONV_A - 1, SEQ_BLOCK, D_A), lambda s, _: (e, 0, s, 0)),
            pl.BlockSpec((None, CONV_B - 1, SEQ_BLOCK, D_B), lambda s, _: (e, 0, s, 0)),
            _resident((d, D_IN_EVEN), lambda s, _: (0, 0)),
            pl.BlockSpec((None, CONV_A, D_A), lambda s, _: (e, 0, 0)),
            vec(D_A), vec(D_A), vec(D_A),
            pl.BlockSpec((None, CONV_B, D_B), lambda s, _: (e, 0, 0)),
            _resident((D_A + D_B, d), lambda s, _: (0, 0)),
        ],
        out_specs=[
            pl.BlockSpec((t, SEQ_BLOCK, d), lambda s, _: (0, s, 0)),
            pl.BlockSpec((CONV_A - 1, SEQ_BLOCK, D_A), lambda s, _: (0, s, 0)),
            pl.BlockSpec((CONV_B - 1, SEQ_BLOCK, D_B), lambda s, _: (0, s, 0)),
        ],
        out_shape=[
            jax.ShapeDtypeStruct(x.shape, F32),
            jax.ShapeDtypeStruct((CONV_A - 1, ns, D_A), F32),
            jax.ShapeDtypeStruct((CONV_B - 1, ns, D_B), F32),
        ],
        compiler_params=_params(),
        name=f"even_sample_{l}",
    )(x, mod, norm_g, hist_a, hist_b, w_in, caw, cab, lng, lnb, cbw, w_out)


def _odd_prompt_kernel(x_ref, mod_ref, g_ref, w_in_ref, b_in_ref, lng_ref, lnb_ref, ws_ref,
                       sbias_ref, w_out_ref, o_ref, cv_ref):
    b = pl.program_id(0)
    t = pl.program_id(1)
    rows = x_ref.shape[0]
    x = x_ref[...]
    shift = mod_ref[0, pl.ds(b, 1), :]
    scale = mod_ref[1, pl.ds(b, 1), :]
    gate = mod_ref[2, pl.ds(b, 1), :]
    h = _rms_mod(x, g_ref[0:1, :], shift, scale).astype(BF16)
    z = jax.nn.gelu(_dot(h, w_in_ref[...]) + b_in_ref[...])
    u = z[:, 0:D_C]
    v = _layernorm(z[:, D_C:2 * D_C], lng_ref[...], lnb_ref[...])

    @pl.when(t == pl.num_programs(1) - 1)
    def _():
        cv_ref[...] = v[rows - CHUNK:rows, :]

    v_bf = v.astype(BF16)
    causal = (lax.broadcasted_iota(jnp.int32, (CHUNK, CHUNK), 0)
              >= lax.broadcasted_iota(jnp.int32, (CHUNK, CHUNK), 1))
    ws = [jnp.where(causal, ws_ref[hd], 0.0).astype(BF16) for hd in range(C_HEADS)]
    s_rows = []
    for c in range(rows // CHUNK):
        heads = [
            _dot(ws[hd], v_bf[c * CHUNK:(c + 1) * CHUNK, hd * C_HEAD_DIM:(hd + 1) * C_HEAD_DIM])
            for hd in range(C_HEADS)
        ]
        s_rows.append(jnp.concatenate(heads, axis=-1) + sbias_ref[...])
    s = jnp.concatenate(s_rows, axis=0)
    y = _dot((u * s).astype(BF16), w_out_ref[...])
    o_ref[...] = x + gate * y


def _odd_prompt_call(x, mod, norm_g, w_in, b_in, lng, lnb, w_s, sbias, w_out, l, casts):
    nb, seq, d = x.shape
    o = l // 2
    n_prompt_blk = (mod.shape[2] - nb) // nb
    n_tiles = seq // ROW_TILE
    c_in, c_out, c_shapes, c_args = _cast_specs(casts, nb, n_tiles)
    outs = pl.pallas_call(
        _with_weight_casts(_odd_prompt_kernel, 10, 2, len(casts)),
        grid=(nb, n_tiles),
        in_specs=[
            pl.BlockSpec((None, ROW_TILE, d), lambda b, t: (b, t, 0)),
            pl.BlockSpec((None, N_MOD, nb, d), lambda b, t: (l, 0, n_prompt_blk, 0)),
            pl.BlockSpec((None, 2, d), lambda b, t: (l, 0, 0)),
            _resident((d, 2 * D_C), lambda b, t: (0, 0)),
            pl.BlockSpec((None, 1, 2 * D_C), lambda b, t: (o, 0, 0)),
            pl.BlockSpec((None, 1, D_C), lambda b, t: (o, 0, 0)),
            pl.BlockSpec((None, 1, D_C), lambda b, t: (o, 0, 0)),
            pl.BlockSpec((None, C_HEADS, CHUNK, CHUNK), lambda b, t: (o, 0, 0, 0)),
            pl.BlockSpec((None, CHUNK, D_C), lambda b, t: (o, 0, 0)),
            _resident((D_C, d), lambda b, t: (0, 0)),
        ] + c_in,
        out_specs=[
            pl.BlockSpec((None, ROW_TILE, d), lambda b, t: (b, t, 0)),
            pl.BlockSpec((None, CHUNK, D_C), lambda b, t: (b, 0, 0)),
        ] + c_out,
        out_shape=[
            jax.ShapeDtypeStruct(x.shape, F32),
            jax.ShapeDtypeStruct((nb, CHUNK, D_C), F32),
        ] + c_shapes,
        compiler_params=_params(),
        name=f"odd_prompt_{l}",
    )(x, mod, norm_g, w_in, b_in, lng, lnb, w_s, sbias, w_out, *c_args)
    return outs[0], outs[1], outs[2:]


def _odd_sample_kernel(x_ref, mod_ref, g_ref, w_in_ref, b_in_ref, lng_ref, lnb_ref, wm_ref,
                       sbias_ref, w_out_ref, o_ref, cv_ref):
    x = x_ref[...]
    t, s, d = x.shape
    h = _rms_mod(x, g_ref[0:1, :], mod_ref[0][None], mod_ref[1][None])
    z = jax.nn.gelu(_dot(h.reshape(t * s, d).astype(BF16), w_in_ref[...]) + b_in_ref[...])
    z = z.reshape(t, s, 2 * D_C)
    u = z[:, :, 0:D_C]
    v = _layernorm(z[:, :, D_C:2 * D_C], lng_ref[...][None], lnb_ref[...][None])
    cv_ref[...] = v
    gated = []
    for i in range(t):
        s_i = jnp.broadcast_to(sbias_ref[i:i + 1, :], (s, D_C))
        for j in range(i + 1):
            s_i = s_i + wm_ref[i, j:j + 1, :] * v[j]
        gated.append(u[i] * s_i)
    us = jnp.stack(gated, axis=0).reshape(t * s, D_C).astype(BF16)
    y = _dot(us, w_out_ref[...]).reshape(t, s, d)
    o_ref[...] = x + mod_ref[2][None] * y


def _odd_sample_call(x, mod, norm_g, w_in, b_in, lng, lnb, wm, sbias, w_out, l):
    t, ns, d = x.shape
    o = l // 2
    return pl.pallas_call(
        _odd_sample_kernel,
        grid=(ns // SEQ_BLOCK, 1),
        in_specs=[
            pl.BlockSpec((t, SEQ_BLOCK, d), lambda s, _: (0, s, 0)),
            pl.BlockSpec((None, N_MOD, SEQ_BLOCK, d), lambda s, _: (l, 0, s, 0)),
            pl.BlockSpec((None, 2, d), lambda s, _: (l, 0, 0)),
            _resident((d, 2 * D_C), lambda s, _: (0, 0)),
            pl.BlockSpec((None, 1, 2 * D_C), lambda s, _: (o, 0, 0)),
            pl.BlockSpec((None, 1, D_C), lambda s, _: (o, 0, 0)),
            pl.BlockSpec((None, 1, D_C), lambda s, _: (o, 0, 0)),
            pl.BlockSpec((None, t, t, D_C), lambda s, _: (o, 0, 0, 0)),
            pl.BlockSpec((None, t, D_C), lambda s, _: (o, 0, 0)),
            _resident((D_C, d), lambda s, _: (0, 0)),
        ],
        out_specs=[
            pl.BlockSpec((t, SEQ_BLOCK, d), lambda s, _: (0, s, 0)),
            pl.BlockSpec((t, SEQ_BLOCK, D_C), lambda s, _: (0, s, 0)),
        ],
        out_shape=[
            jax.ShapeDtypeStruct(x.shape, F32),
            jax.ShapeDtypeStruct((t, ns, D_C), F32),
        ],
        compiler_params=_params(),
        name=f"odd_sample_{l}",
    )(x, mod, norm_g, w_in, b_in, lng, lnb, wm, sbias, w_out)


def kernel(x_prompt, x_sample, state_conv_a, state_conv_b, c_prompt, c_sample, w_in_ab, conv_a_w, conv_a_b, ln_a_g, ln_a_b, conv_b_w, w_out_ab, w_in_c, b_in_c, ln_v_g, ln_v_b, w_s, b_s, w_out_c, w_ada, b_ada, norm_g, w_ff1, w_ff2, final_g):
    dec_seq = x_sample.shape[1]
    n_even, n_odd = w_in_ab.shape[0], w_in_c.shape[0]

    mix_in_bf = w_in_ab[0].astype(BF16)
    mix_out_bf = w_out_ab[0].astype(BF16)

    cab3 = conv_a_b.reshape(n_even, 1, D_A)
    lnag3 = ln_a_g.reshape(n_even, 1, D_A)
    lnab3 = ln_a_b.reshape(n_even, 1, D_A)
    binc3 = b_in_c.reshape(n_odd, 1, 2 * D_C)
    lnvg3 = ln_v_g.reshape(n_odd, 1, D_C)
    lnvb3 = ln_v_b.reshape(n_odd, 1, D_C)
    fg2 = final_g.reshape(1, D_MODEL)

    sbias = jnp.repeat(jnp.swapaxes(b_s, 1, 2), C_HEAD_DIM, axis=2)
    wm = jnp.repeat(jnp.transpose(w_s[:, :, :dec_seq, :dec_seq], (0, 2, 3, 1)),
                    C_HEAD_DIM, axis=3)

    mod = _ada_call(jnp.concatenate([c_sample, c_prompt], axis=0), w_ada, b_ada)

    xp = x_prompt
    xs = jnp.transpose(x_sample, (1, 0, 2))
    ha = jnp.transpose(state_conv_a, (0, 2, 1, 3))
    hb = jnp.transpose(state_conv_b, (0, 2, 1, 3))

    a_p, b_p, v_p, a_s, b_s_out, v_s = [], [], [], [], [], []
    for l in range(DEPTH):
        final = l == DEPTH - 1
        ff_casts = [(w_ff1, l), (w_ff2, l)]
        if l % 2 == 0:
            xp, sa, sb, (ff1_bf, ff2_bf) = _even_prompt_call(
                xp, mod, norm_g, mix_in_bf, conv_a_w, cab3, lnag3, lnab3, conv_b_w, mix_out_bf, l, ff_casts)
            a_p.append(sa)
            b_p.append(sb)
            xs, sa, sb = _even_sample_call(xs, mod, norm_g, ha, hb, mix_in_bf, conv_a_w, cab3, lnag3,
                                           lnab3, conv_b_w, mix_out_bf, l)
            a_s.append(jnp.transpose(sa, (1, 0, 2)))
            b_s_out.append(jnp.transpose(sb, (1, 0, 2)))
        else:
            xp, cv, (ff1_bf, ff2_bf) = _odd_prompt_call(
                xp, mod, norm_g, mix_in_bf, binc3, lnvg3, lnvb3, w_s, sbias, mix_out_bf, l, ff_casts)
            v_p.append(cv)
            xs, cv = _odd_sample_call(xs, mod, norm_g, mix_in_bf, binc3, lnvg3, lnvb3, wm,
                                      sbias[:, :dec_seq], mix_out_bf, l)
            v_s.append(jnp.transpose(cv, (1, 0, 2)))
        if final:
            mix_casts = []
        elif l % 2 == 0:
            mix_casts = [(w_in_c, l // 2), (w_out_c, l // 2)]
        else:
            mix_casts = [(w_in_ab, (l + 1) // 2), (w_out_ab, (l + 1) // 2)]
        xp, next_mix = _mlp_prompt_call(xp, mod, norm_g, ff1_bf, ff2_bf, fg2, l, final, mix_casts)
        xs = _mlp_sample_call(xs, mod, norm_g, ff1_bf, ff2_bf, fg2, l, final)
        if next_mix:
            mix_in_bf, mix_out_bf = next_mix

    return (xp, jnp.transpose(xs, (1, 0, 2)), jnp.stack(a_p), jnp.stack(a_s), jnp.stack(b_p),
            jnp.stack(b_s_out), jnp.stack(v_p), jnp.stack(v_s))
```

```python
import functools

import jax
import jax.numpy as jnp
from jax import lax
from jax.experimental import pallas as pl
from jax.experimental.pallas import tpu as pltpu

D_MODEL = 1024
DEPTH = 4
D_A = 512
D_B = 512
CONV_A = 31
CONV_B = 3
D_IN_EVEN = 2 * D_A + 3 * D_B
D_C = 1024
C_HEADS = 8
C_HEAD_DIM = 128
CHUNK = 128
D_FF = 4096
N_MOD = 6
EPS = 1e-6

BF16 = jnp.bfloat16
F32 = jnp.float32

VMEM_LIMIT_BYTES = 56 * 1024 * 1024
SUBLANES = 8
ROW_TILE = 1024
ODD_SUB_ROWS = 512
MLP_ROW_TILE = 1024
SEQ_BLOCK = 64
FF_CHUNK = 1024
A_HIST_ROWS = 32
B_HIST_ROWS = 8
LANES = 128
CONV_SUB_ROWS = 256
CONV_ROW_BLOCK = 128


def _params():
    return pltpu.CompilerParams(
        dimension_semantics=("arbitrary", "arbitrary"),
        vmem_limit_bytes=VMEM_LIMIT_BYTES)


def _resident(shape, index_map):
    return pl.BlockSpec(shape, index_map, pipeline_mode=pl.Buffered(1))


def _with_weight_casts(body, n_in, n_out, n_cast):
    def wrapped(*refs):
        ins = refs[:n_in]
        cast_src = refs[n_in:n_in + n_cast]
        outs = refs[n_in + n_cast:n_in + n_cast + n_out]
        cast_dst = refs[n_in + n_cast + n_out:n_in + 2 * n_cast + n_out]
        scratch = refs[n_in + 2 * n_cast + n_out:]
        for src, dst in zip(cast_src, cast_dst):
            dst[...] = src[...].astype(BF16)
        body(*ins, *outs, *scratch)
    return wrapped


def _cast_specs(casts, n_outer, n_inner):
    in_specs, out_specs, out_shapes, args = [], [], [], []
    for w, layer in casts:
        _, r, c = w.shape
        rb = r // (n_outer * n_inner)
        in_specs.append(pl.BlockSpec((None, rb, c), lambda b, t, layer=layer: (layer, b * n_inner + t, 0)))
        out_specs.append(pl.BlockSpec((rb, c), lambda b, t: (b * n_inner + t, 0)))
        out_shapes.append(jax.ShapeDtypeStruct((r, c), BF16))
        args.append(w)
    return in_specs, out_specs, out_shapes, args


def _rms(x, g):
    ms = jnp.mean(x * x, axis=-1, keepdims=True)
    return x * lax.rsqrt(ms + EPS) * g


def _rms_mod(x, g, shift, scale):
    ms = jnp.mean(x * x, axis=-1, keepdims=True)
    return x * lax.rsqrt(ms + EPS) * (g * (1.0 + scale)) + shift


def _layernorm(x, g, b):
    mu = jnp.mean(x, axis=-1, keepdims=True)
    xc = x - mu
    var = jnp.mean(xc * xc, axis=-1, keepdims=True)
    return xc * lax.rsqrt(var + EPS) * g + b


def _silu(x):
    return x * jax.nn.sigmoid(x)


def _dot(a, b):
    return jnp.dot(a, b, preferred_element_type=F32)


def _ada_kernel(c_ref, w_ref, b_ref, o_ref):
    c = c_ref[...]
    o_ref[...] = _dot(_silu(c).astype(BF16), w_ref[...].astype(BF16)) + b_ref[...]


def _ada_call(c_all, w_ada, b_ada):
    n = c_all.shape[0]
    b4 = b_ada.reshape(DEPTH, N_MOD, 1, D_MODEL)
    return pl.pallas_call(
        _ada_kernel,
        grid=(DEPTH, N_MOD),
        in_specs=[
            pl.BlockSpec((n, D_MODEL), lambda l, j: (0, 0)),
            pl.BlockSpec((None, D_MODEL, D_MODEL), lambda l, j: (l, 0, j)),
            pl.BlockSpec((None, None, 1, D_MODEL), lambda l, j: (l, j, 0, 0)),
        ],
        out_specs=pl.BlockSpec((None, None, n, D_MODEL), lambda l, j: (l, j, 0, 0)),
        out_shape=jax.ShapeDtypeStruct((DEPTH, N_MOD, n, D_MODEL), F32),
        compiler_params=_params(),
        name="adaln",
    )(c_all, w_ada, b4)


def _mlp_body(x2d, h_bf, gate, w1_ref, w2_ref):
    acc = None
    for j in range(D_FF // FF_CHUNK):
        cols = slice(j * FF_CHUNK, (j + 1) * FF_CHUNK)
        hid = _dot(h_bf, w1_ref[:, cols])
        hid = jnp.square(jnp.maximum(hid, 0.0)).astype(BF16)
        part = _dot(hid, w2_ref[cols, :])
        acc = part if acc is None else acc + part
    return x2d + gate * acc


def _mlp_prompt_kernel(x_ref, mod_ref, g_ref, w1_ref, w2_ref, fg_ref, o_ref, *, final):
    b = pl.program_id(0)
    x = x_ref[...]
    shift = mod_ref[3, pl.ds(b, 1), :]
    scale = mod_ref[4, pl.ds(b, 1), :]
    gate = mod_ref[5, pl.ds(b, 1), :]
    h = _rms_mod(x, g_ref[1:2, :], shift, scale).astype(BF16)
    y = _mlp_body(x, h, gate, w1_ref, w2_ref)
    if final:
        y = _rms(y, fg_ref[...])
    o_ref[...] = y


def _mlp_sample_kernel(x_ref, mod_ref, g_ref, w1_ref, w2_ref, fg_ref, o_ref, *, final):
    x = x_ref[...]
    t, s, d = x.shape
    h = _rms_mod(x, g_ref[1:2, :], mod_ref[3][None], mod_ref[4][None])
    h = h.reshape(t * s, d).astype(BF16)
    gate = jnp.broadcast_to(mod_ref[5][None], (t, s, d)).reshape(t * s, d)
    y = _mlp_body(x.reshape(t * s, d), h, gate, w1_ref, w2_ref)
    if final:
        y = _rms(y, fg_ref[...])
    o_ref[...] = y.reshape(t, s, d)


def _mlp_prompt_call(x, mod, norm_g, w1, w2, final_g, l, final, casts):
    nb, seq, d = x.shape
    n_prompt_blk = (mod.shape[2] - nb) // nb
    n_tiles = seq // MLP_ROW_TILE
    c_in, c_out, c_shapes, c_args = _cast_specs(casts, nb, n_tiles)
    outs = pl.pallas_call(
        _with_weight_casts(functools.partial(_mlp_prompt_kernel, final=final), 6, 1, len(casts)),
        grid=(nb, n_tiles),
        in_specs=[
            pl.BlockSpec((None, MLP_ROW_TILE, d), lambda b, t: (b, t, 0)),
            pl.BlockSpec((None, N_MOD, nb, d), lambda b, t: (l, 0, n_prompt_blk, 0)),
            pl.BlockSpec((None, 2, d), lambda b, t: (l, 0, 0)),
            _resident((d, D_FF), lambda b, t: (0, 0)),
            _resident((D_FF, d), lambda b, t: (0, 0)),
            pl.BlockSpec((1, d), lambda b, t: (0, 0)),
        ] + c_in,
        out_specs=[pl.BlockSpec((None, MLP_ROW_TILE, d), lambda b, t: (b, t, 0))] + c_out,
        out_shape=[jax.ShapeDtypeStruct(x.shape, F32)] + c_shapes,
        compiler_params=_params(),
        name=f"mlp_prompt_{l}",
    )(x, mod, norm_g, w1, w2, final_g, *c_args)
    return outs[0], outs[1:]


def _mlp_sample_call(x, mod, norm_g, w1, w2, final_g, l, final):
    t, ns, d = x.shape
    return pl.pallas_call(
        functools.partial(_mlp_sample_kernel, final=final),
        grid=(ns // SEQ_BLOCK, 1),
        in_specs=[
            pl.BlockSpec((t, SEQ_BLOCK, d), lambda s, _: (0, s, 0)),
            pl.BlockSpec((None, N_MOD, SEQ_BLOCK, d), lambda s, _: (l, 0, s, 0)),
            pl.BlockSpec((None, 2, d), lambda s, _: (l, 0, 0)),
            _resident((d, D_FF), lambda s, _: (0, 0)),
            _resident((D_FF, d), lambda s, _: (0, 0)),
            pl.BlockSpec((1, d), lambda s, _: (0, 0)),
        ],
        out_specs=pl.BlockSpec((t, SEQ_BLOCK, d), lambda s, _: (0, s, 0)),
        out_shape=jax.ShapeDtypeStruct(x.shape, F32),
        compiler_params=_params(),
        name=f"mlp_sample_{l}",
    )(x, mod, norm_g, w1, w2, final_g)


def _even_prompt_kernel(x_ref, mod_ref, g_ref, w_in_ref, caw_ref, cab_ref, lng_ref, lnb_ref,
                        cbw_ref, w_out_ref, o_ref, sa_ref, sb_ref, abuf, bbuf, shbuf, cbuf):
    b = pl.program_id(0)
    t = pl.program_id(1)
    rows = x_ref.shape[0]
    sub = CONV_SUB_ROWS
    ext = sub + A_HIST_ROWS
    first_a = A_HIST_ROWS - (CONV_A - 1)
    first_b = B_HIST_ROWS - (CONV_B - 1)

    @pl.when(t == 0)
    def _():
        abuf[0:A_HIST_ROWS, :] = jnp.zeros((A_HIST_ROWS, D_A), F32)
        bbuf[0:B_HIST_ROWS, :] = jnp.zeros((B_HIST_ROWS, D_B), F32)

    shift = mod_ref[0, pl.ds(b, 1), :]
    scale = mod_ref[1, pl.ds(b, 1), :]
    gate = mod_ref[2, pl.ds(b, 1), :]

    for j in range(rows // sub):
        r0 = j * sub
        x = x_ref[r0:r0 + sub, :]
        h = _rms_mod(x, g_ref[0:1, :], shift, scale).astype(BF16)
        z = _dot(h, w_in_ref[...])

        abuf[A_HIST_ROWS + r0:A_HIST_ROWS + r0 + sub, :] = (
            z[:, 0:D_A] * jax.nn.sigmoid(z[:, D_A:2 * D_A]))
        a_ext = abuf[r0:r0 + ext, :]
        for s in range(1, SUBLANES):
            shbuf[j % 2, s - 1] = pltpu.roll(a_ext, ext - s, axis=0)
        for c in range(D_A // LANES):
            lanes = slice(c * LANES, (c + 1) * LANES)
            for rb in range(sub // CONV_ROW_BLOCK):
                acc = jnp.broadcast_to(cab_ref[:, lanes], (CONV_ROW_BLOCK, LANES))
                for k in range(CONV_A):
                    q, s = divmod(first_a + k, SUBLANES)
                    start = q * SUBLANES + rb * CONV_ROW_BLOCK
                    if s == 0:
                        tap = abuf[r0 + start:r0 + start + CONV_ROW_BLOCK, lanes]
                    else:
                        tap = shbuf[j % 2, s - 1, start:start + CONV_ROW_BLOCK, lanes]
                    acc = acc + caw_ref[k:k + 1, lanes] * tap
                cbuf[r0 + rb * CONV_ROW_BLOCK:r0 + (rb + 1) * CONV_ROW_BLOCK, lanes] = acc
        a_out = _silu(_layernorm(cbuf[r0:r0 + sub, :], lng_ref[...], lnb_ref[...]))

        bbuf[B_HIST_ROWS + r0:B_HIST_ROWS + r0 + sub, :] = z[:, 4 * D_A:5 * D_A] * z[:, 2 * D_A:3 * D_A]
        accb = cbw_ref[0:1, :] * bbuf[pl.ds(r0 + first_b, sub), :]
        for k in range(1, CONV_B):
            accb = accb + cbw_ref[k:k + 1, :] * bbuf[pl.ds(r0 + first_b + k, sub), :]
        b_out = z[:, 3 * D_A:4 * D_A] * accb

        y = _dot(jnp.concatenate([a_out, b_out], axis=-1).astype(BF16), w_out_ref[...])
        o_ref[r0:r0 + sub, :] = x + gate * y

    @pl.when(t == pl.num_programs(1) - 1)
    def _():
        sa_ref[...] = abuf[pl.ds(rows + first_a, CONV_A - 1), :]
        sb_ref[...] = bbuf[pl.ds(rows + first_b, CONV_B - 1), :]

    abuf[0:A_HIST_ROWS, :] = abuf[rows:rows + A_HIST_ROWS, :]
    bbuf[0:B_HIST_ROWS, :] = bbuf[rows:rows + B_HIST_ROWS, :]


def _even_prompt_call(x, mod, norm_g, w_in, caw, cab, lng, lnb, cbw, w_out, l, casts):
    nb, seq, d = x.shape
    e = l // 2
    n_prompt_blk = (mod.shape[2] - nb) // nb
    n_tiles = seq // ROW_TILE
    c_in, c_out, c_shapes, c_args = _cast_specs(casts, nb, n_tiles)
    vec = lambda width: pl.BlockSpec((None, 1, width), lambda b, t: (e, 0, 0))
    outs = pl.pallas_call(
        _with_weight_casts(_even_prompt_kernel, 10, 3, len(casts)),
        grid=(nb, n_tiles),
        in_specs=[
            pl.BlockSpec((None, ROW_TILE, d), lambda b, t: (b, t, 0)),
            pl.BlockSpec((None, N_MOD, nb, d), lambda b, t: (l, 0, n_prompt_blk, 0)),
            pl.BlockSpec((None, 2, d), lambda b, t: (l, 0, 0)),
            _resident((d, D_IN_EVEN), lambda b, t: (0, 0)),
            pl.BlockSpec((None, CONV_A, D_A), lambda b, t: (e, 0, 0)),
            vec(D_A), vec(D_A), vec(D_A),
            pl.BlockSpec((None, CONV_B, D_B), lambda b, t: (e, 0, 0)),
            _resident((D_A + D_B, d), lambda b, t: (0, 0)),
        ] + c_in,
        out_specs=[
            pl.BlockSpec((None, ROW_TILE, d), lambda b, t: (b, t, 0)),
            pl.BlockSpec((None, CONV_A - 1, D_A), lambda b, t: (b, 0, 0)),
            pl.BlockSpec((None, CONV_B - 1, D_B), lambda b, t: (b, 0, 0)),
        ] + c_out,
        out_shape=[
            jax.ShapeDtypeStruct(x.shape, F32),
            jax.ShapeDtypeStruct((nb, CONV_A - 1, D_A), F32),
            jax.ShapeDtypeStruct((nb, CONV_B - 1, D_B), F32),
        ] + c_shapes,
        scratch_shapes=[
            pltpu.VMEM((ROW_TILE + A_HIST_ROWS, D_A), F32),
            pltpu.VMEM((ROW_TILE + B_HIST_ROWS, D_B), F32),
            pltpu.VMEM((2, SUBLANES - 1, CONV_SUB_ROWS + A_HIST_ROWS, D_A), F32),
            pltpu.VMEM((ROW_TILE, D_A), F32),
        ],
        compiler_params=_params(),
        name=f"even_prompt_{l}",
    )(x, mod, norm_g, w_in, caw, cab, lng, lnb, cbw, w_out, *c_args)
    return outs[0], outs[1], outs[2], outs[3:]


def _even_sample_kernel(x_ref, mod_ref, g_ref, ha_ref, hb_ref, w_in_ref, caw_ref, cab_ref,
                        lng_ref, lnb_ref, cbw_ref, w_out_ref, o_ref, sa_ref, sb_ref):
    x = x_ref[...]
    t, s, d = x.shape
    h = _rms_mod(x, g_ref[0:1, :], mod_ref[0][None], mod_ref[1][None])
    z = _dot(h.reshape(t * s, d).astype(BF16), w_in_ref[...]).reshape(t, s, D_IN_EVEN)

    a = z[:, :, 0:D_A] * jax.nn.sigmoid(z[:, :, D_A:2 * D_A])
    a_ext = jnp.concatenate([ha_ref[...], a], axis=0)
    acc = jnp.broadcast_to(cab_ref[...][None], (t, s, D_A))
    for k in range(CONV_A):
        acc = acc + caw_ref[k:k + 1, :][None] * a_ext[k:k + t]
    a_out = _silu(_layernorm(acc, lng_ref[...][None], lnb_ref[...][None]))
    sa_ref[...] = a_ext[t:]

    bx = z[:, :, 4 * D_A:5 * D_A] * z[:, :, 2 * D_A:3 * D_A]
    b_ext = jnp.concatenate([hb_ref[...], bx], axis=0)
    accb = cbw_ref[0:1, :][None] * b_ext[0:t]
    for k in range(1, CONV_B):
        accb = accb + cbw_ref[k:k + 1, :][None] * b_ext[k:k + t]
    b_out = z[:, :, 3 * D_A:4 * D_A] * accb
    sb_ref[...] = b_ext[t:]

    cat = jnp.concatenate([a_out, b_out], axis=-1).reshape(t * s, D_A + D_B).astype(BF16)
    y = _dot(cat, w_out_ref[...]).reshape(t, s, d)
    o_ref[...] = x + mod_ref[2][None] * y


def _even_sample_call(x, mod, norm_g, hist_a, hist_b, w_in, caw, cab, lng, lnb, cbw, w_out, l):
    t, ns, d = x.shape
    e = l // 2
    vec = lambda width: pl.BlockSpec((None, 1, width), lambda s, _: (e, 0, 0))
    return pl.pallas_call(
        _even_sample_kernel,
        grid=(ns // SEQ_BLOCK, 1),
        in_specs=[
            pl.BlockSpec((t, SEQ_BLOCK, d), lambda s, _: (0, s, 0)),
            pl.BlockSpec((None, N_MOD, SEQ_BLOCK, d), lambda s, _: (l, 0, s, 0)),
            pl.BlockSpec((None, 2, d), lambda s, _: (l, 0, 0)),
            pl.BlockSpec((None, CONV_A - 1, SEQ_BLOCK, D_A), lambda s, _: (e, 0, s, 0)),
            pl.BlockSpec((None, CONV_B - 1, SEQ_BLOCK, D_B), lambda s, _: (e, 0, s, 0)),
            _resident((d, D_IN_EVEN), lambda s, _: (0, 0)),
            pl.BlockSpec((None, CONV_A, D_A), lambda s, _: (e, 0, 0)),
            vec(D_A), vec(D_A), vec(D_A),
            pl.BlockSpec((None, CONV_B, D_B), lambda s, _: (e, 0, 0)),
            _resident((D_A + D_B, d), lambda s, _: (0, 0)),
        ],
        out_specs=[
            pl.BlockSpec((t, SEQ_BLOCK, d), lambda s, _: (0, s, 0)),
            pl.BlockSpec((CONV_A - 1, SEQ_BLOCK, D_A), lambda s, _: (0, s, 0)),
            pl.BlockSpec((CONV_B - 1, SEQ_BLOCK, D_B), lambda s, _: (0, s, 0)),
        ],
        out_shape=[
            jax.ShapeDtypeStruct(x.shape, F32),
            jax.ShapeDtypeStruct((CONV_A - 1, ns, D_A), F32),
            jax.ShapeDtypeStruct((CONV_B - 1, ns, D_B), F32),
        ],
        compiler_params=_params(),
        name=f"even_sample_{l}",
    )(x, mod, norm_g, hist_a, hist_b, w_in, caw, cab, lng, lnb, cbw, w_out)


def _odd_prompt_kernel(x_ref, mod_ref, g_ref, w_in_ref, b_in_ref, lng_ref, lnb_ref, ws_ref,
                       sbias_ref, w_out_ref, o_ref, cv_ref):
    b = pl.program_id(0)
    t = pl.program_id(1)
    rows = x_ref.shape[0]
    sub = ODD_SUB_ROWS
    shift = mod_ref[0, pl.ds(b, 1), :]
    scale = mod_ref[1, pl.ds(b, 1), :]
    gate = mod_ref[2, pl.ds(b, 1), :]
    causal = (lax.broadcasted_iota(jnp.int32, (CHUNK, CHUNK), 0)
              >= lax.broadcasted_iota(jnp.int32, (CHUNK, CHUNK), 1))
    ws = [jnp.where(causal, ws_ref[hd], 0.0).astype(BF16) for hd in range(C_HEADS)]

    for j in range(rows // sub):
        r0 = j * sub
        x = x_ref[r0:r0 + sub, :]
        h = _rms_mod(x, g_ref[0:1, :], shift, scale).astype(BF16)
        z = jax.nn.gelu(_dot(h, w_in_ref[...]) + b_in_ref[...])
        u = z[:, 0:D_C]
        v = _layernorm(z[:, D_C:2 * D_C], lng_ref[...], lnb_ref[...])

        if r0 + sub == rows:
            @pl.when(t == pl.num_programs(1) - 1)
            def _():
                cv_ref[...] = v[sub - CHUNK:sub, :]

        v_bf = v.astype(BF16)
        s_rows = []
        for c in range(sub // CHUNK):
            heads = [
                _dot(ws[hd], v_bf[c * CHUNK:(c + 1) * CHUNK, hd * C_HEAD_DIM:(hd + 1) * C_HEAD_DIM])
                for hd in range(C_HEADS)
            ]
            s_rows.append(jnp.concatenate(heads, axis=-1) + sbias_ref[...])
        s = jnp.concatenate(s_rows, axis=0)
        y = _dot((u * s).astype(BF16), w_out_ref[...])
        o_ref[r0:r0 + sub, :] = x + gate * y


def _odd_prompt_call(x, mod, norm_g, w_in, b_in, lng, lnb, w_s, sbias, w_out, l, casts):
    nb, seq, d = x.shape
    o = l // 2
    n_prompt_blk = (mod.shape[2] - nb) // nb
    n_tiles = seq // ROW_TILE
    c_in, c_out, c_shapes, c_args = _cast_specs(casts, nb, n_tiles)
    outs = pl.pallas_call(
        _with_weight_casts(_odd_prompt_kernel, 10, 2, len(casts)),
        grid=(nb, n_tiles),
        in_specs=[
            pl.BlockSpec((None, ROW_TILE, d), lambda b, t: (b, t, 0)),
            pl.BlockSpec((None, N_MOD, nb, d), lambda b, t: (l, 0, n_prompt_blk, 0)),
            pl.BlockSpec((None, 2, d), lambda b, t: (l, 0, 0)),
            _resident((d, 2 * D_C), lambda b, t: (0, 0)),
            pl.BlockSpec((None, 1, 2 * D_C), lambda b, t: (o, 0, 0)),
            pl.BlockSpec((None, 1, D_C), lambda b, t: (o, 0, 0)),
            pl.BlockSpec((None, 1, D_C), lambda b, t: (o, 0, 0)),
            pl.BlockSpec((None, C_HEADS, CHUNK, CHUNK), lambda b, t: (o, 0, 0, 0)),
            pl.BlockSpec((None, CHUNK, D_C), lambda b, t: (o, 0, 0)),
            _resident((D_C, d), lambda b, t: (0, 0)),
        ] + c_in,
        out_specs=[
            pl.BlockSpec((None, ROW_TILE, d), lambda b, t: (b, t, 0)),
            pl.BlockSpec((None, CHUNK, D_C), lambda b, t: (b, 0, 0)),
        ] + c_out,
        out_shape=[
            jax.ShapeDtypeStruct(x.shape, F32),
            jax.ShapeDtypeStruct((nb, CHUNK, D_C), F32),
        ] + c_shapes,
        compiler_params=_params(),
        name=f"odd_prompt_{l}",
    )(x, mod, norm_g, w_in, b_in, lng, lnb, w_s, sbias, w_out, *c_args)
    return outs[0], outs[1], outs[2:]


def _odd_sample_kernel(x_ref, mod_ref, g_ref, w_in_ref, b_in_ref, lng_ref, lnb_ref, wm_ref,
                       sbias_ref, w_out_ref, o_ref, cv_ref):
    x = x_ref[...]
    t, s, d = x.shape
    h = _rms_mod(x, g_ref[0:1, :], mod_ref[0][None], mod_ref[1][None])
    z = jax.nn.gelu(_dot(h.reshape(t * s, d).astype(BF16), w_in_ref[...]) + b_in_ref[...])
    z = z.reshape(t, s, 2 * D_C)
    u = z[:, :, 0:D_C]
    v = _layernorm(z[:, :, D_C:2 * D_C], lng_ref[...][None], lnb_ref[...][None])
    cv_ref[...] = v
    gated = []
    for i in range(t):
        s_i = jnp.broadcast_to(sbias_ref[i:i + 1, :], (s, D_C))
        for j in range(i + 1):
            s_i = s_i + wm_ref[i, j:j + 1, :] * v[j]
        gated.append(u[i] * s_i)
    us = jnp.stack(gated, axis=0).reshape(t * s, D_C).astype(BF16)
    y = _dot(us, w_out_ref[...]).reshape(t, s, d)
    o_ref[...] = x + mod_ref[2][None] * y


def _odd_sample_call(x, mod, norm_g, w_in, b_in, lng, lnb, wm, sbias, w_out, l):
    t, ns, d = x.shape
    o = l // 2
    return pl.pallas_call(
        _odd_sample_kernel,
        grid=(ns // SEQ_BLOCK, 1),
        in_specs=[
            pl.BlockSpec((t, SEQ_BLOCK, d), lambda s, _: (0, s, 0)),
            pl.BlockSpec((None, N_MOD, SEQ_BLOCK, d), lambda s, _: (l, 0, s, 0)),
            pl.BlockSpec((None, 2, d), lambda s, _: (l, 0, 0)),
            _resident((d, 2 * D_C), lambda s, _: (0, 0)),
            pl.BlockSpec((None, 1, 2 * D_C), lambda s, _: (o, 0, 0)),
            pl.BlockSpec((None, 1, D_C), lambda s, _: (o, 0, 0)),
            pl.BlockSpec((None, 1, D_C), lambda s, _: (o, 0, 0)),
            pl.BlockSpec((None, t, t, D_C), lambda s, _: (o, 0, 0, 0)),
            pl.BlockSpec((None, t, D_C), lambda s, _: (o, 0, 0)),
            _resident((D_C, d), lambda s, _: (0, 0)),
        ],
        out_specs=[
            pl.BlockSpec((t, SEQ_BLOCK, d), lambda s, _: (0, s, 0)),
            pl.BlockSpec((t, SEQ_BLOCK, D_C), lambda s, _: (0, s, 0)),
        ],
        out_shape=[
            jax.ShapeDtypeStruct(x.shape, F32),
            jax.ShapeDtypeStruct((t, ns, D_C), F32),
        ],
        compiler_params=_params(),
        name=f"odd_sample_{l}",
    )(x, mod, norm_g, w_in, b_in, lng, lnb, wm, sbias, w_out)


def kernel(x_prompt, x_sample, state_conv_a, state_conv_b, c_prompt, c_sample, w_in_ab, conv_a_w, conv_a_b, ln_a_g, ln_a_b, conv_b_w, w_out_ab, w_in_c, b_in_c, ln_v_g, ln_v_b, w_s, b_s, w_out_c, w_ada, b_ada, norm_g, w_ff1, w_ff2, final_g):
    dec_seq = x_sample.shape[1]
    n_even, n_odd = w_in_ab.shape[0], w_in_c.shape[0]

    mix_in_bf = w_in_ab[0].astype(BF16)
    mix_out_bf = w_out_ab[0].astype(BF16)

    cab3 = conv_a_b.reshape(n_even, 1, D_A)
    lnag3 = ln_a_g.reshape(n_even, 1, D_A)
    lnab3 = ln_a_b.reshape(n_even, 1, D_A)
    binc3 = b_in_c.reshape(n_odd, 1, 2 * D_C)
    lnvg3 = ln_v_g.reshape(n_odd, 1, D_C)
    lnvb3 = ln_v_b.reshape(n_odd, 1, D_C)
    fg2 = final_g.reshape(1, D_MODEL)

    sbias = jnp.repeat(jnp.swapaxes(b_s, 1, 2), C_HEAD_DIM, axis=2)
    wm = jnp.repeat(jnp.transpose(w_s[:, :, :dec_seq, :dec_seq], (0, 2, 3, 1)),
                    C_HEAD_DIM, axis=3)

    mod = _ada_call(jnp.concatenate([c_sample, c_prompt], axis=0), w_ada, b_ada)

    xp = x_prompt
    xs = jnp.transpose(x_sample, (1, 0, 2))
    ha = jnp.transpose(state_conv_a, (0, 2, 1, 3))
    hb = jnp.transpose(state_conv_b, (0, 2, 1, 3))

    a_p, b_p, v_p, a_s, b_s_out, v_s = [], [], [], [], [], []
    for l in range(DEPTH):
        final = l == DEPTH - 1
        ff_casts = [(w_ff1, l), (w_ff2, l)]
        if l % 2 == 0:
            xp, sa, sb, (ff1_bf, ff2_bf) = _even_prompt_call(
                xp, mod, norm_g, mix_in_bf, conv_a_w, cab3, lnag3, lnab3, conv_b_w, mix_out_bf, l, ff_casts)
            a_p.append(sa)
            b_p.append(sb)
            xs, sa, sb = _even_sample_call(xs, mod, norm_g, ha, hb, mix_in_bf, conv_a_w, cab3, lnag3,
                                           lnab3, conv_b_w, mix_out_bf, l)
            a_s.append(jnp.transpose(sa, (1, 0, 2)))
            b_s_out.append(jnp.transpose(sb, (1, 0, 2)))
        else:
            xp, cv, (ff1_bf, ff2_bf) = _odd_prompt_call(
                xp, mod, norm_g, mix_in_bf, binc3, lnvg3, lnvb3, w_s, sbias, mix_out_bf, l, ff_casts)
            v_p.append(cv)
            xs, cv = _odd_sample_call(xs, mod, norm_g, mix_in_bf, binc3, lnvg3, lnvb3, wm,
                                      sbias[:, :dec_seq], mix_out_bf, l)
            v_s.append(jnp.transpose(cv, (1, 0, 2)))
        if final:
            mix_casts = []
        elif l % 2 == 0:
            mix_casts = [(w_in_c, l // 2), (w_out_c, l // 2)]
        else:
            mix_casts = [(w_in_ab, (l + 1) // 2), (w_out_ab, (l + 1) // 2)]
        xp, next_mix = _mlp_prompt_call(xp, mod, norm_g, ff1_bf, ff2_bf, fg2, l, final, mix_casts)
        xs = _mlp_sample_call(xs, mod, norm_g, ff1_bf, ff2_bf, fg2, l, final)
        if next_mix:
            mix_in_bf, mix_out_bf = next_mix

    return (xp, jnp.transpose(xs, (1, 0, 2)), jnp.stack(a_p), jnp.stack(a_s), jnp.stack(b_p),
            jnp.stack(b_s_out), jnp.stack(v_p), jnp.stack(v_s))
```

```python
import functools

import jax
import jax.numpy as jnp
from jax import lax
from jax.experimental import pallas as pl
from jax.experimental.pallas import tpu as pltpu

D_MODEL = 1024
DEPTH = 4
D_A = 512
D_B = 512
CONV_A = 31
CONV_B = 3
D_IN_EVEN = 2 * D_A + 3 * D_B
D_C = 1024
C_HEADS = 8
C_HEAD_DIM = 128
CHUNK = 128
D_FF = 4096
N_MOD = 6
EPS = 1e-6

BF16 = jnp.bfloat16
F32 = jnp.float32

VMEM_LIMIT_BYTES = 56 * 1024 * 1024
SUBLANES = 8
ROW_TILE = 1024
ODD_SUB_ROWS = 512
MLP_ROW_TILE = 1024
SEQ_BLOCK = 64
FF_CHUNK = 1024
A_HIST_ROWS = 32
B_HIST_ROWS = 8
LANES = 128
CONV_SUB_ROWS = 256
CONV_ROW_BLOCK = 128


def _params():
    return pltpu.CompilerParams(
        dimension_semantics=("arbitrary", "arbitrary"),
        vmem_limit_bytes=VMEM_LIMIT_BYTES)


def _resident(shape, index_map):
    return pl.BlockSpec(shape, index_map, pipeline_mode=pl.Buffered(1))


def _with_weight_casts(body, n_in, n_out, n_cast):
    def wrapped(*refs):
        ins = refs[:n_in]
        cast_src = refs[n_in:n_in + n_cast]
        outs = refs[n_in + n_cast:n_in + n_cast + n_out]
        cast_dst = refs[n_in + n_cast + n_out:n_in + 2 * n_cast + n_out]
        scratch = refs[n_in + 2 * n_cast + n_out:]
        for src, dst in zip(cast_src, cast_dst):
            dst[...] = src[...].astype(BF16)
        body(*ins, *outs, *scratch)
    return wrapped


def _cast_specs(casts, n_outer, n_inner):
    in_specs, out_specs, out_shapes, args = [], [], [], []
    for w, layer in casts:
        _, r, c = w.shape
        rb = r // (n_outer * n_inner)
        in_specs.append(pl.BlockSpec((None, rb, c), lambda b, t, layer=layer: (layer, b * n_inner + t, 0)))
        out_specs.append(pl.BlockSpec((rb, c), lambda b, t: (b * n_inner + t, 0)))
        out_shapes.append(jax.ShapeDtypeStruct((r, c), BF16))
        args.append(w)
    return in_specs, out_specs, out_shapes, args


def _rms(x, g):
    ms = jnp.mean(x * x, axis=-1, keepdims=True)
    return x * lax.rsqrt(ms + EPS) * g


def _rms_mod(x, g, shift, scale):
    ms = jnp.mean(x * x, axis=-1, keepdims=True)
    return x * lax.rsqrt(ms + EPS) * (g * (1.0 + scale)) + shift


def _layernorm(x, g, b):
    mu = jnp.mean(x, axis=-1, keepdims=True)
    xc = x - mu
    var = jnp.mean(xc * xc, axis=-1, keepdims=True)
    return xc * lax.rsqrt(var + EPS) * g + b


def _silu(x):
    return x * jax.nn.sigmoid(x)


def _dot(a, b):
    return jnp.dot(a, b, preferred_element_type=F32)


def _ada_kernel(c_ref, w_ref, b_ref, o_ref):
    c = c_ref[...]
    o_ref[...] = _dot(_silu(c).astype(BF16), w_ref[...].astype(BF16)) + b_ref[...]


def _ada_call(c_all, w_ada, b_ada):
    n = c_all.shape[0]
    b4 = b_ada.reshape(DEPTH, N_MOD, 1, D_MODEL)
    return pl.pallas_call(
        _ada_kernel,
        grid=(DEPTH, N_MOD),
        in_specs=[
            pl.BlockSpec((n, D_MODEL), lambda l, j: (0, 0)),
            pl.BlockSpec((None, D_MODEL, D_MODEL), lambda l, j: (l, 0, j)),
            pl.BlockSpec((None, None, 1, D_MODEL), lambda l, j: (l, j, 0, 0)),
        ],
        out_specs=pl.BlockSpec((None, None, n, D_MODEL), lambda l, j: (l, j, 0, 0)),
        out_shape=jax.ShapeDtypeStruct((DEPTH, N_MOD, n, D_MODEL), F32),
        compiler_params=_params(),
        name="adaln",
    )(c_all, w_ada, b4)


def _mlp_body(x2d, h_bf, gate, w1_ref, w2_ref):
    acc = None
    for j in range(D_FF // FF_CHUNK):
        cols = slice(j * FF_CHUNK, (j + 1) * FF_CHUNK)
        hid = _dot(h_bf, w1_ref[:, cols])
        hid = jnp.square(jnp.maximum(hid, 0.0)).astype(BF16)
        part = _dot(hid, w2_ref[cols, :])
        acc = part if acc is None else acc + part
    return x2d + gate * acc


def _mlp_prompt_kernel(x_ref, mod_ref, g_ref, w1_ref, w2_ref, fg_ref, o_ref, *, final):
    b = pl.program_id(0)
    x = x_ref[...]
    shift = mod_ref[3, pl.ds(b, 1), :]
    scale = mod_ref[4, pl.ds(b, 1), :]
    gate = mod_ref[5, pl.ds(b, 1), :]
    h = _rms_mod(x, g_ref[1:2, :], shift, scale).astype(BF16)
    y = _mlp_body(x, h, gate, w1_ref, w2_ref)
    if final:
        y = _rms(y, fg_ref[...])
    o_ref[...] = y


def _mlp_sample_kernel(x_ref, mod_ref, g_ref, w1_ref, w2_ref, fg_ref, o_ref, hbuf, acc, *, final):
    j = pl.program_id(0)
    t, s, d = x_ref.shape

    @pl.when(j == 0)
    def _():
        h = _rms_mod(x_ref[...], g_ref[1:2, :], mod_ref[3][None], mod_ref[4][None])
        hbuf[...] = h.reshape(t * s, d).astype(BF16)

    hid = _dot(hbuf[...], w1_ref[...])
    hid = jnp.square(jnp.maximum(hid, 0.0)).astype(BF16)
    part = _dot(hid, w2_ref[...])

    @pl.when(j == 0)
    def _():
        acc[...] = part

    @pl.when(j > 0)
    def _():
        acc[...] = acc[...] + part

    @pl.when(j == pl.num_programs(0) - 1)
    def _():
        y = x_ref[...] + mod_ref[5][None] * acc[...].reshape(t, s, d)
        if final:
            y = _rms(y, fg_ref[...])
        o_ref[...] = y


def _mlp_prompt_call(x, mod, norm_g, w1, w2, final_g, l, final, casts):
    nb, seq, d = x.shape
    n_prompt_blk = (mod.shape[2] - nb) // nb
    n_tiles = seq // MLP_ROW_TILE
    c_in, c_out, c_shapes, c_args = _cast_specs(casts, nb, n_tiles)
    outs = pl.pallas_call(
        _with_weight_casts(functools.partial(_mlp_prompt_kernel, final=final), 6, 1, len(casts)),
        grid=(nb, n_tiles),
        in_specs=[
            pl.BlockSpec((None, MLP_ROW_TILE, d), lambda b, t: (b, t, 0)),
            pl.BlockSpec((None, N_MOD, nb, d), lambda b, t: (l, 0, n_prompt_blk, 0)),
            pl.BlockSpec((None, 2, d), lambda b, t: (l, 0, 0)),
            _resident((d, D_FF), lambda b, t: (0, 0)),
            _resident((D_FF, d), lambda b, t: (0, 0)),
            pl.BlockSpec((1, d), lambda b, t: (0, 0)),
        ] + c_in,
        out_specs=[pl.BlockSpec((None, MLP_ROW_TILE, d), lambda b, t: (b, t, 0))] + c_out,
        out_shape=[jax.ShapeDtypeStruct(x.shape, F32)] + c_shapes,
        compiler_params=_params(),
        name=f"mlp_prompt_{l}",
    )(x, mod, norm_g, w1, w2, final_g, *c_args)
    return outs[0], outs[1:]


def _mlp_sample_call(x, mod, norm_g, w1, w2, final_g, l, final):
    t, ns, d = x.shape
    return pl.pallas_call(
        functools.partial(_mlp_sample_kernel, final=final),
        grid=(D_FF // FF_CHUNK, 1),
        in_specs=[
            _resident((t, ns, d), lambda j, _: (0, 0, 0)),
            _resident((None, N_MOD, ns, d), lambda j, _: (l, 0, 0, 0)),
            pl.BlockSpec((None, 2, d), lambda j, _: (l, 0, 0)),
            pl.BlockSpec((d, FF_CHUNK), lambda j, _: (0, j)),
            pl.BlockSpec((FF_CHUNK, d), lambda j, _: (j, 0)),
            pl.BlockSpec((1, d), lambda j, _: (0, 0)),
        ],
        out_specs=pl.BlockSpec((t, ns, d), lambda j, _: (0, 0, 0)),
        out_shape=jax.ShapeDtypeStruct(x.shape, F32),
        scratch_shapes=[
            pltpu.VMEM((t * ns, d), BF16),
            pltpu.VMEM((t * ns, d), F32),
        ],
        compiler_params=_params(),
        name=f"mlp_sample_{l}",
    )(x, mod, norm_g, w1, w2, final_g)


def _even_prompt_kernel(x_ref, mod_ref, g_ref, w_in_ref, caw_ref, cab_ref, lng_ref, lnb_ref,
                        cbw_ref, w_out_ref, o_ref, sa_ref, sb_ref, abuf, bbuf, shbuf, cbuf):
    b = pl.program_id(0)
    t = pl.program_id(1)
    rows = x_ref.shape[0]
    sub = CONV_SUB_ROWS
    ext = sub + A_HIST_ROWS
    first_a = A_HIST_ROWS - (CONV_A - 1)
    first_b = B_HIST_ROWS - (CONV_B - 1)
    n_lane_tiles = D_A // LANES

    @pl.when(t == 0)
    def _():
        abuf[:, 0:A_HIST_ROWS, :] = jnp.zeros((n_lane_tiles, A_HIST_ROWS, LANES), F32)
        bbuf[0:B_HIST_ROWS, :] = jnp.zeros((B_HIST_ROWS, D_B), F32)

    shift = mod_ref[0, pl.ds(b, 1), :]
    scale = mod_ref[1, pl.ds(b, 1), :]
    gate = mod_ref[2, pl.ds(b, 1), :]

    for j in range(rows // sub):
        r0 = j * sub
        x = x_ref[r0:r0 + sub, :]
        h = _rms_mod(x, g_ref[0:1, :], shift, scale).astype(BF16)
        z = _dot(h, w_in_ref[...])

        a = z[:, 0:D_A] * jax.nn.sigmoid(z[:, D_A:2 * D_A])
        for c in range(n_lane_tiles):
            abuf[c, A_HIST_ROWS + r0:A_HIST_ROWS + r0 + sub, :] = a[:, c * LANES:(c + 1) * LANES]
        for c in range(n_lane_tiles):
            a_ext = abuf[c, r0:r0 + ext, :]
            for s in range(1, SUBLANES):
                shbuf[j % 2, s - 1, c] = pltpu.roll(a_ext, ext - s, axis=0)
        for c in range(n_lane_tiles):
            lanes = slice(c * LANES, (c + 1) * LANES)
            for rb in range(sub // CONV_ROW_BLOCK):
                acc = jnp.broadcast_to(cab_ref[:, lanes], (CONV_ROW_BLOCK, LANES))
                for k in range(CONV_A):
                    q, s = divmod(first_a + k, SUBLANES)
                    start = q * SUBLANES + rb * CONV_ROW_BLOCK
                    if s == 0:
                        tap = abuf[c, r0 + start:r0 + start + CONV_ROW_BLOCK, :]
                    else:
                        tap = shbuf[j % 2, s - 1, c, start:start + CONV_ROW_BLOCK, :]
                    acc = acc + caw_ref[k:k + 1, lanes] * tap
                cbuf[c, r0 + rb * CONV_ROW_BLOCK:r0 + (rb + 1) * CONV_ROW_BLOCK, :] = acc
        conv_out = jnp.concatenate([cbuf[c, r0:r0 + sub, :] for c in range(n_lane_tiles)], axis=-1)
        a_out = _silu(_layernorm(conv_out, lng_ref[...], lnb_ref[...]))

        bbuf[B_HIST_ROWS + r0:B_HIST_ROWS + r0 + sub, :] = z[:, 4 * D_A:5 * D_A] * z[:, 2 * D_A:3 * D_A]
        accb = cbw_ref[0:1, :] * bbuf[pl.ds(r0 + first_b, sub), :]
        for k in range(1, CONV_B):
            accb = accb + cbw_ref[k:k + 1, :] * bbuf[pl.ds(r0 + first_b + k, sub), :]
        b_out = z[:, 3 * D_A:4 * D_A] * accb

        y = _dot(jnp.concatenate([a_out, b_out], axis=-1).astype(BF16), w_out_ref[...])
        o_ref[r0:r0 + sub, :] = x + gate * y

    @pl.when(t == pl.num_programs(1) - 1)
    def _():
        for c in range(n_lane_tiles):
            sa_ref[:, c * LANES:(c + 1) * LANES] = abuf[c, pl.ds(rows + first_a, CONV_A - 1), :]
        sb_ref[...] = bbuf[pl.ds(rows + first_b, CONV_B - 1), :]

    abuf[:, 0:A_HIST_ROWS, :] = abuf[:, rows:rows + A_HIST_ROWS, :]
    bbuf[0:B_HIST_ROWS, :] = bbuf[rows:rows + B_HIST_ROWS, :]


def _even_prompt_call(x, mod, norm_g, w_in, caw, cab, lng, lnb, cbw, w_out, l, casts):
    nb, seq, d = x.shape
    e = l // 2
    n_prompt_blk = (mod.shape[2] - nb) // nb
    n_tiles = seq // ROW_TILE
    c_in, c_out, c_shapes, c_args = _cast_specs(casts, nb, n_tiles)
    vec = lambda width: pl.BlockSpec((None, 1, width), lambda b, t: (e, 0, 0))
    outs = pl.pallas_call(
        _with_weight_casts(_even_prompt_kernel, 10, 3, len(casts)),
        grid=(nb, n_tiles),
        in_specs=[
            pl.BlockSpec((None, ROW_TILE, d), lambda b, t: (b, t, 0)),
            pl.BlockSpec((None, N_MOD, nb, d), lambda b, t: (l, 0, n_prompt_blk, 0)),
            pl.BlockSpec((None, 2, d), lambda b, t: (l, 0, 0)),
            _resident((d, D_IN_EVEN), lambda b, t: (0, 0)),
            pl.BlockSpec((None, CONV_A, D_A), lambda b, t: (e, 0, 0)),
            vec(D_A), vec(D_A), vec(D_A),
            pl.BlockSpec((None, CONV_B, D_B), lambda b, t: (e, 0, 0)),
            _resident((D_A + D_B, d), lambda b, t: (0, 0)),
        ] + c_in,
        out_specs=[
            pl.BlockSpec((None, ROW_TILE, d), lambda b, t: (b, t, 0)),
            pl.BlockSpec((None, CONV_A - 1, D_A), lambda b, t: (b, 0, 0)),
            pl.BlockSpec((None, CONV_B - 1, D_B), lambda b, t: (b, 0, 0)),
        ] + c_out,
        out_shape=[
            jax.ShapeDtypeStruct(x.shape, F32),
            jax.ShapeDtypeStruct((nb, CONV_A - 1, D_A), F32),
            jax.ShapeDtypeStruct((nb, CONV_B - 1, D_B), F32),
        ] + c_shapes,
        scratch_shapes=[
            pltpu.VMEM((D_A // LANES, ROW_TILE + A_HIST_ROWS, LANES), F32),
            pltpu.VMEM((ROW_TILE + B_HIST_ROWS, D_B), F32),
            pltpu.VMEM((2, SUBLANES - 1, D_A // LANES, CONV_SUB_ROWS + A_HIST_ROWS, LANES), F32),
            pltpu.VMEM((D_A // LANES, ROW_TILE, LANES), F32),
        ],
        compiler_params=_params(),
        name=f"even_prompt_{l}",
    )(x, mod, norm_g, w_in, caw, cab, lng, lnb, cbw, w_out, *c_args)
    return outs[0], outs[1], outs[2], outs[3:]


def _even_sample_kernel(x_ref, mod_ref, g_ref, ha_ref, hb_ref, w_in_ref, caw_ref, cab_ref,
                        lng_ref, lnb_ref, cbw_ref, w_out_ref, o_ref, sa_ref, sb_ref):
    x = x_ref[...]
    t, s, d = x.shape
    h = _rms_mod(x, g_ref[0:1, :], mod_ref[0][None], mod_ref[1][None])
    z = _dot(h.reshape(t * s, d).astype(BF16), w_in_ref[...]).reshape(t, s, D_IN_EVEN)

    a = z[:, :, 0:D_A] * jax.nn.sigmoid(z[:, :, D_A:2 * D_A])
    a_ext = jnp.concatenate([ha_ref[...], a], axis=0)
    acc = jnp.broadcast_to(cab_ref[...][None], (t, s, D_A))
    for k in range(CONV_A):
        acc = acc + caw_ref[k:k + 1, :][None] * a_ext[k:k + t]
    a_out = _silu(_layernorm(acc, lng_ref[...][None], lnb_ref[...][None]))
    sa_ref[...] = a_ext[t:]

    bx = z[:, :, 4 * D_A:5 * D_A] * z[:, :, 2 * D_A:3 * D_A]
    b_ext = jnp.concatenate([hb_ref[...], bx], axis=0)
    accb = cbw_ref[0:1, :][None] * b_ext[0:t]
    for k in range(1, CONV_B):
        accb = accb + cbw_ref[k:k + 1, :][None] * b_ext[k:k + t]
    b_out = z[:, :, 3 * D_A:4 * D_A] * accb
    sb_ref[...] = b_ext[t:]

    cat = jnp.concatenate([a_out, b_out], axis=-1).reshape(t * s, D_A + D_B).astype(BF16)
    y = _dot(cat, w_out_ref[...]).reshape(t, s, d)
    o_ref[...] = x + mod_ref[2][None] * y


def _even_sample_call(x, mod, norm_g, hist_a, hist_b, w_in, caw, cab, lng, lnb, cbw, w_out, l):
    t, ns, d = x.shape
    e = l // 2
    vec = lambda width: pl.BlockSpec((None, 1, width), lambda s, _: (e, 0, 0))
    return pl.pallas_call(
        _even_sample_kernel,
        grid=(ns // SEQ_BLOCK, 1),
        in_specs=[
            pl.BlockSpec((t, SEQ_BLOCK, d), lambda s, _: (0, s, 0)),
            pl.BlockSpec((None, N_MOD, SEQ_BLOCK, d), lambda s, _: (l, 0, s, 0)),
            pl.BlockSpec((None, 2, d), lambda s, _: (l, 0, 0)),
            pl.BlockSpec((None, CONV_A - 1, SEQ_BLOCK, D_A), lambda s, _: (e, 0, s, 0)),
            pl.BlockSpec((None, CONV_B - 1, SEQ_BLOCK, D_B), lambda s, _: (e, 0, s, 0)),
            _resident((d, D_IN_EVEN), lambda s, _: (0, 0)),
            pl.BlockSpec((None, CONV_A, D_A), lambda s, _: (e, 0, 0)),
            vec(D_A), vec(D_A), vec(D_A),
            pl.BlockSpec((None, CONV_B, D_B), lambda s, _: (e, 0, 0)),
            _resident((D_A + D_B, d), lambda s, _: (0, 0)),
        ],
        out_specs=[
            pl.BlockSpec((t, SEQ_BLOCK, d), lambda s, _: (0, s, 0)),
            pl.BlockSpec((CONV_A - 1, SEQ_BLOCK, D_A), lambda s, _: (0, s, 0)),
            pl.BlockSpec((CONV_B - 1, SEQ_BLOCK, D_B), lambda s, _: (0, s, 0)),
        ],
        out_shape=[
            jax.ShapeDtypeStruct(x.shape, F32),
            jax.ShapeDtypeStruct((CONV_A - 1, ns, D_A), F32),
            jax.ShapeDtypeStruct((CONV_B - 1, ns, D_B), F32),
        ],
        compiler_params=_params(),
        name=f"even_sample_{l}",
    )(x, mod, norm_g, hist_a, hist_b, w_in, caw, cab, lng, lnb, cbw, w_out)


def _odd_prompt_kernel(x_ref, mod_ref, g_ref, w_in_ref, b_in_ref, lng_ref, lnb_ref, ws_ref,
                       sbias_ref, w_out_ref, o_ref, cv_ref):
    b = pl.program_id(0)
    t = pl.program_id(1)
    rows = x_ref.shape[0]
    sub = ODD_SUB_ROWS
    shift = mod_ref[0, pl.ds(b, 1), :]
    scale = mod_ref[1, pl.ds(b, 1), :]
    gate = mod_ref[2, pl.ds(b, 1), :]
    causal = (lax.broadcasted_iota(jnp.int32, (CHUNK, CHUNK), 0)
              >= lax.broadcasted_iota(jnp.int32, (CHUNK, CHUNK), 1))
    ws = [jnp.where(causal, ws_ref[hd], 0.0).astype(BF16) for hd in range(C_HEADS)]

    for j in range(rows // sub):
        r0 = j * sub
        x = x_ref[r0:r0 + sub, :]
        h = _rms_mod(x, g_ref[0:1, :], shift, scale).astype(BF16)
        z = jax.nn.gelu(_dot(h, w_in_ref[...]) + b_in_ref[...])
        u = z[:, 0:D_C]
        v = _layernorm(z[:, D_C:2 * D_C], lng_ref[...], lnb_ref[...])

        if r0 + sub == rows:
            @pl.when(t == pl.num_programs(1) - 1)
            def _():
                cv_ref[...] = v[sub - CHUNK:sub, :]

        v_bf = v.astype(BF16)
        s_rows = []
        for c in range(sub // CHUNK):
            heads = [
                _dot(ws[hd], v_bf[c * CHUNK:(c + 1) * CHUNK, hd * C_HEAD_DIM:(hd + 1) * C_HEAD_DIM])
                for hd in range(C_HEADS)
            ]
            s_rows.append(jnp.concatenate(heads, axis=-1) + sbias_ref[...])
        s = jnp.concatenate(s_rows, axis=0)
        y = _dot((u * s).astype(BF16), w_out_ref[...])
        o_ref[r0:r0 + sub, :] = x + gate * y


def _odd_prompt_call(x, mod, norm_g, w_in, b_in, lng, lnb, w_s, sbias, w_out, l, casts):
    nb, seq, d = x.shape
    o = l // 2
    n_prompt_blk = (mod.shape[2] - nb) // nb
    n_tiles = seq // ROW_TILE
    c_in, c_out, c_shapes, c_args = _cast_specs(casts, nb, n_tiles)
    outs = pl.pallas_call(
        _with_weight_casts(_odd_prompt_kernel, 10, 2, len(casts)),
        grid=(nb, n_tiles),
        in_specs=[
            pl.BlockSpec((None, ROW_TILE, d), lambda b, t: (b, t, 0)),
            pl.BlockSpec((None, N_MOD, nb, d), lambda b, t: (l, 0, n_prompt_blk, 0)),
            pl.BlockSpec((None, 2, d), lambda b, t: (l, 0, 0)),
            _resident((d, 2 * D_C), lambda b, t: (0, 0)),
            pl.BlockSpec((None, 1, 2 * D_C), lambda b, t: (o, 0, 0)),
            pl.BlockSpec((None, 1, D_C), lambda b, t: (o, 0, 0)),
            pl.BlockSpec((None, 1, D_C), lambda b, t: (o, 0, 0)),
            pl.BlockSpec((None, C_HEADS, CHUNK, CHUNK), lambda b, t: (o, 0, 0, 0)),
            pl.BlockSpec((None, CHUNK, D_C), lambda b, t: (o, 0, 0)),
            _resident((D_C, d), lambda b, t: (0, 0)),
        ] + c_in,
        out_specs=[
            pl.BlockSpec((None, ROW_TILE, d), lambda b, t: (b, t, 0)),
            pl.BlockSpec((None, CHUNK, D_C), lambda b, t: (b, 0, 0)),
        ] + c_out,
        out_shape=[
            jax.ShapeDtypeStruct(x.shape, F32),
            jax.ShapeDtypeStruct((nb, CHUNK, D_C), F32),
        ] + c_shapes,
        compiler_params=_params(),
        name=f"odd_prompt_{l}",
    )(x, mod, norm_g, w_in, b_in, lng, lnb, w_s, sbias, w_out, *c_args)
    return outs[0], outs[1], outs[2:]


def _odd_sample_kernel(x_ref, mod_ref, g_ref, w_in_ref, b_in_ref, lng_ref, lnb_ref, wm_ref,
                       sbias_ref, w_out_ref, o_ref, cv_ref):
    x = x_ref[...]
    t, s, d = x.shape
    h = _rms_mod(x, g_ref[0:1, :], mod_ref[0][None], mod_ref[1][None])
    z = jax.nn.gelu(_dot(h.reshape(t * s, d).astype(BF16), w_in_ref[...]) + b_in_ref[...])
    z = z.reshape(t, s, 2 * D_C)
    u = z[:, :, 0:D_C]
    v = _layernorm(z[:, :, D_C:2 * D_C], lng_ref[...][None], lnb_ref[...][None])
    cv_ref[...] = v
    gated = []
    for i in range(t):
        s_i = jnp.broadcast_to(sbias_ref[i:i + 1, :], (s, D_C))
        for j in range(i + 1):
            s_i = s_i + wm_ref[i, j:j + 1, :] * v[j]
        gated.append(u[i] * s_i)
    us = jnp.stack(gated, axis=0).reshape(t * s, D_C).astype(BF16)
    y = _dot(us, w_out_ref[...]).reshape(t, s, d)
    o_ref[...] = x + mod_ref[2][None] * y


def _odd_sample_call(x, mod, norm_g, w_in, b_in, lng, lnb, wm, sbias, w_out, l):
    t, ns, d = x.shape
    o = l // 2
    return pl.pallas_call(
        _odd_sample_kernel,
        grid=(ns // SEQ_BLOCK, 1),
        in_specs=[
            pl.BlockSpec((t, SEQ_BLOCK, d), lambda s, _: (0, s, 0)),
            pl.BlockSpec((None, N_MOD, SEQ_BLOCK, d), lambda s, _: (l, 0, s, 0)),
            pl.BlockSpec((None, 2, d), lambda s, _: (l, 0, 0)),
            _resident((d, 2 * D_C), lambda s, _: (0, 0)),
            pl.BlockSpec((None, 1, 2 * D_C), lambda s, _: (o, 0, 0)),
            pl.BlockSpec((None, 1, D_C), lambda s, _: (o, 0, 0)),
            pl.BlockSpec((None, 1, D_C), lambda s, _: (o, 0, 0)),
            pl.BlockSpec((None, t, t, D_C), lambda s, _: (o, 0, 0, 0)),
            pl.BlockSpec((None, t, D_C), lambda s, _: (o, 0, 0)),
            _resident((D_C, d), lambda s, _: (0, 0)),
        ],
        out_specs=[
            pl.BlockSpec((t, SEQ_BLOCK, d), lambda s, _: (0, s, 0)),
            pl.BlockSpec((t, SEQ_BLOCK, D_C), lambda s, _: (0, s, 0)),
        ],
        out_shape=[
            jax.ShapeDtypeStruct(x.shape, F32),
            jax.ShapeDtypeStruct((t, ns, D_C), F32),
        ],
        compiler_params=_params(),
        name=f"odd_sample_{l}",
    )(x, mod, norm_g, w_in, b_in, lng, lnb, wm, sbias, w_out)


def kernel(x_prompt, x_sample, state_conv_a, state_conv_b, c_prompt, c_sample, w_in_ab, conv_a_w, conv_a_b, ln_a_g, ln_a_b, conv_b_w, w_out_ab, w_in_c, b_in_c, ln_v_g, ln_v_b, w_s, b_s, w_out_c, w_ada, b_ada, norm_g, w_ff1, w_ff2, final_g):
    dec_seq = x_sample.shape[1]
    n_even, n_odd = w_in_ab.shape[0], w_in_c.shape[0]

    mix_in_bf = w_in_ab[0].astype(BF16)
    mix_out_bf = w_out_ab[0].astype(BF16)

    cab3 = conv_a_b.reshape(n_even, 1, D_A)
    lnag3 = ln_a_g.reshape(n_even, 1, D_A)
    lnab3 = ln_a_b.reshape(n_even, 1, D_A)
    binc3 = b_in_c.reshape(n_odd, 1, 2 * D_C)
    lnvg3 = ln_v_g.reshape(n_odd, 1, D_C)
    lnvb3 = ln_v_b.reshape(n_odd, 1, D_C)
    fg2 = final_g.reshape(1, D_MODEL)

    sbias = jnp.repeat(jnp.swapaxes(b_s, 1, 2), C_HEAD_DIM, axis=2)
    wm = jnp.repeat(jnp.transpose(w_s[:, :, :dec_seq, :dec_seq], (0, 2, 3, 1)),
                    C_HEAD_DIM, axis=3)

    mod = _ada_call(jnp.concatenate([c_sample, c_prompt], axis=0), w_ada, b_ada)

    xp = x_prompt
    xs = jnp.transpose(x_sample, (1, 0, 2))
    ha = jnp.transpose(state_conv_a, (0, 2, 1, 3))
    hb = jnp.transpose(state_conv_b, (0, 2, 1, 3))

    a_p, b_p, v_p, a_s, b_s_out, v_s = [], [], [], [], [], []
    for l in range(DEPTH):
        final = l == DEPTH - 1
        ff_casts = [(w_ff1, l), (w_ff2, l)]
        if l % 2 == 0:
            xp, sa, sb, (ff1_bf, ff2_bf) = _even_prompt_call(
                xp, mod, norm_g, mix_in_bf, conv_a_w, cab3, lnag3, lnab3, conv_b_w, mix_out_bf, l, ff_casts)
            a_p.append(sa)
            b_p.append(sb)
            xs, sa, sb = _even_sample_call(xs, mod, norm_g, ha, hb, mix_in_bf, conv_a_w, cab3, lnag3,
                                           lnab3, conv_b_w, mix_out_bf, l)
            a_s.append(sa)
            b_s_out.append(sb)
        else:
            xp, cv, (ff1_bf, ff2_bf) = _odd_prompt_call(
                xp, mod, norm_g, mix_in_bf, binc3, lnvg3, lnvb3, w_s, sbias, mix_out_bf, l, ff_casts)
            v_p.append(cv)
            xs, cv = _odd_sample_call(xs, mod, norm_g, mix_in_bf, binc3, lnvg3, lnvb3, wm,
                                      sbias[:, :dec_seq], mix_out_bf, l)
            v_s.append(cv)
        if final:
            mix_casts = []
        elif l % 2 == 0:
            mix_casts = [(w_in_c, l // 2), (w_out_c, l // 2)]
        else:
            mix_casts = [(w_in_ab, (l + 1) // 2), (w_out_ab, (l + 1) // 2)]
        xp, next_mix = _mlp_prompt_call(xp, mod, norm_g, ff1_bf, ff2_bf, fg2, l, final, mix_casts)
        xs = _mlp_sample_call(xs, mod, norm_g, ff1_bf, ff2_bf, fg2, l, final)
        if next_mix:
            mix_in_bf, mix_out_bf = next_mix

    to_seq_major = lambda parts: jnp.transpose(jnp.stack(parts), (0, 2, 1, 3))
    return (xp, jnp.transpose(xs, (1, 0, 2)), jnp.stack(a_p), to_seq_major(a_s), jnp.stack(b_p),
            to_seq_major(b_s_out), jnp.stack(v_p), to_seq_major(v_s))
```

```python
import functools

import jax
import jax.numpy as jnp
from jax import lax
from jax.experimental import pallas as pl
from jax.experimental.pallas import tpu as pltpu

D_MODEL = 1024
DEPTH = 4
D_A = 512
D_B = 512
CONV_A = 31
CONV_B = 3
D_IN_EVEN = 2 * D_A + 3 * D_B
D_C = 1024
C_HEADS = 8
C_HEAD_DIM = 128
CHUNK = 128
D_FF = 4096
N_MOD = 6
EPS = 1e-6

BF16 = jnp.bfloat16
F32 = jnp.float32

VMEM_LIMIT_BYTES = 56 * 1024 * 1024
SUBLANES = 8
ROW_TILE = 1024
ODD_SUB_ROWS = 512
MLP_ROW_TILE = 512
SEQ_BLOCK = 64
FF_CHUNK = 1024
ADA_GROUP = 3
A_HIST_ROWS = 32
B_HIST_ROWS = 8
LANES = 128
CONV_SUB_ROWS = 256
CONV_ROW_BLOCK = 128


def _params():
    return pltpu.CompilerParams(
        dimension_semantics=("arbitrary", "arbitrary"),
        vmem_limit_bytes=VMEM_LIMIT_BYTES)


def _resident(shape, index_map):
    return pl.BlockSpec(shape, index_map, pipeline_mode=pl.Buffered(1))


def _with_weight_casts(body, n_in, n_out, n_cast):
    def wrapped(*refs):
        ins = refs[:n_in]
        cast_src = refs[n_in:n_in + n_cast]
        outs = refs[n_in + n_cast:n_in + n_cast + n_out]
        cast_dst = refs[n_in + n_cast + n_out:n_in + 2 * n_cast + n_out]
        scratch = refs[n_in + 2 * n_cast + n_out:]
        for src, dst in zip(cast_src, cast_dst):
            dst[...] = src[...].astype(BF16)
        body(*ins, *outs, *scratch)
    return wrapped


def _cast_specs(casts, n_outer, n_inner):
    in_specs, out_specs, out_shapes, args = [], [], [], []
    for w, layer in casts:
        _, r, c = w.shape
        rb = r // (n_outer * n_inner)
        in_specs.append(pl.BlockSpec((None, rb, c), lambda b, t, layer=layer: (layer, b * n_inner + t, 0)))
        out_specs.append(pl.BlockSpec((rb, c), lambda b, t: (b * n_inner + t, 0)))
        out_shapes.append(jax.ShapeDtypeStruct((r, c), BF16))
        args.append(w)
    return in_specs, out_specs, out_shapes, args


def _rms(x, g):
    ms = jnp.mean(x * x, axis=-1, keepdims=True)
    return x * lax.rsqrt(ms + EPS) * g


def _rms_mod(x, g, shift, scale):
    ms = jnp.mean(x * x, axis=-1, keepdims=True)
    return x * lax.rsqrt(ms + EPS) * (g * (1.0 + scale)) + shift


def _layernorm(x, g, b):
    mu = jnp.mean(x, axis=-1, keepdims=True)
    xc = x - mu
    var = jnp.mean(xc * xc, axis=-1, keepdims=True)
    return xc * lax.rsqrt(var + EPS) * g + b


def _silu(x):
    return x * jax.nn.sigmoid(x)


def _dot(a, b):
    return jnp.dot(a, b, preferred_element_type=F32)


def _ada_kernel(c_ref, w_ref, b_ref, o_ref):
    c = c_ref[...]
    res = _dot(_silu(c).astype(BF16), w_ref[...].astype(BF16))
    for i in range(ADA_GROUP):
        o_ref[i] = res[:, i * D_MODEL:(i + 1) * D_MODEL] + b_ref[i]


def _ada_call(c_all, w_ada, b_ada):
    n = c_all.shape[0]
    b4 = b_ada.reshape(DEPTH, N_MOD, 1, D_MODEL)
    return pl.pallas_call(
        _ada_kernel,
        grid=(DEPTH, N_MOD // ADA_GROUP),
        in_specs=[
            pl.BlockSpec((n, D_MODEL), lambda l, j: (0, 0)),
            pl.BlockSpec((None, D_MODEL, ADA_GROUP * D_MODEL), lambda l, j: (l, 0, j)),
            pl.BlockSpec((None, ADA_GROUP, 1, D_MODEL), lambda l, j: (l, j, 0, 0)),
        ],
        out_specs=pl.BlockSpec((None, ADA_GROUP, n, D_MODEL), lambda l, j: (l, j, 0, 0)),
        out_shape=jax.ShapeDtypeStruct((DEPTH, N_MOD, n, D_MODEL), F32),
        compiler_params=_params(),
        name="adaln",
    )(c_all, w_ada, b4)


def _mlp_body(x2d, h_bf, gate, w1_ref, w2_ref):
    acc = None
    for j in range(D_FF // FF_CHUNK):
        cols = slice(j * FF_CHUNK, (j + 1) * FF_CHUNK)
        hid = _dot(h_bf, w1_ref[:, cols])
        hid = jnp.square(jnp.maximum(hid, 0.0)).astype(BF16)
        part = _dot(hid, w2_ref[cols, :])
        acc = part if acc is None else acc + part
    return x2d + gate * acc


def _mlp_prompt_kernel(x_ref, mod_ref, g_ref, w1_ref, w2_ref, fg_ref, xs_ref, mods_ref, o_ref, os_ref,
                       *, final):
    b = pl.program_id(0)
    x = x_ref[...]
    shift = mod_ref[3, pl.ds(b, 1), :]
    scale = mod_ref[4, pl.ds(b, 1), :]
    gate = mod_ref[5, pl.ds(b, 1), :]
    h = _rms_mod(x, g_ref[1:2, :], shift, scale).astype(BF16)
    y = _mlp_body(x, h, gate, w1_ref, w2_ref)
    if final:
        y = _rms(y, fg_ref[...])
    o_ref[...] = y

    @pl.when(jnp.logical_and(b == pl.num_programs(0) - 1,
                             pl.program_id(1) == pl.num_programs(1) - 1))
    def _():
        t, s, d = xs_ref.shape
        half = t // 2
        for p in range(0, t, half):
            xs = xs_ref[p:p + half]
            hs = _rms_mod(xs, g_ref[1:2, :], mods_ref[3][None], mods_ref[4][None])
            hs = hs.reshape(half * s, d).astype(BF16)
            gs = jnp.broadcast_to(mods_ref[5][None], (half, s, d)).reshape(half * s, d)
            ys = _mlp_body(xs.reshape(half * s, d), hs, gs, w1_ref, w2_ref)
            if final:
                ys = _rms(ys, fg_ref[...])
            os_ref[p:p + half] = ys.reshape(half, s, d)


def _mlp_call(x, xs, mod, norm_g, w1, w2, final_g, l, final, casts):
    nb, seq, d = x.shape
    ts, ns, _ = xs.shape
    n_prompt_blk = (mod.shape[2] - nb) // nb
    n_tiles = seq // MLP_ROW_TILE
    c_in, c_out, c_shapes, c_args = _cast_specs(casts, nb, n_tiles)
    outs = pl.pallas_call(
        _with_weight_casts(functools.partial(_mlp_prompt_kernel, final=final), 8, 2, len(casts)),
        grid=(nb, n_tiles),
        in_specs=[
            pl.BlockSpec((None, MLP_ROW_TILE, d), lambda b, t: (b, t, 0)),
            pl.BlockSpec((None, N_MOD, nb, d), lambda b, t: (l, 0, n_prompt_blk, 0)),
            pl.BlockSpec((None, 2, d), lambda b, t: (l, 0, 0)),
            _resident((d, D_FF), lambda b, t: (0, 0)),
            _resident((D_FF, d), lambda b, t: (0, 0)),
            pl.BlockSpec((1, d), lambda b, t: (0, 0)),
            _resident((ts, ns, d), lambda b, t: (0, 0, 0)),
            _resident((None, N_MOD, ns, d), lambda b, t: (l, 0, 0, 0)),
        ] + c_in,
        out_specs=[pl.BlockSpec((None, MLP_ROW_TILE, d), lambda b, t: (b, t, 0)),
                   pl.BlockSpec((ts, ns, d), lambda b, t: (0, 0, 0))] + c_out,
        out_shape=[jax.ShapeDtypeStruct(x.shape, F32), jax.ShapeDtypeStruct(xs.shape, F32)] + c_shapes,
        compiler_params=_params(),
        name=f"mlp_{l}",
    )(x, mod, norm_g, w1, w2, final_g, xs, mod, *c_args)
    return outs[0], outs[1], outs[2:]


def _even_prompt_kernel(x_ref, mod_ref, g_ref, w_in_ref, caw_ref, cab_ref, lng_ref, lnb_ref,
                        cbw_ref, w_out_ref, o_ref, sa_ref, sb_ref, abuf, bbuf, shbuf, cbuf):
    b = pl.program_id(0)
    t = pl.program_id(1)
    rows = x_ref.shape[0]
    sub = CONV_SUB_ROWS
    ext = sub + A_HIST_ROWS
    first_a = A_HIST_ROWS - (CONV_A - 1)
    first_b = B_HIST_ROWS - (CONV_B - 1)

    @pl.when(t == 0)
    def _():
        abuf[0:A_HIST_ROWS, :] = jnp.zeros((A_HIST_ROWS, D_A), F32)
        bbuf[0:B_HIST_ROWS, :] = jnp.zeros((B_HIST_ROWS, D_B), F32)

    shift = mod_ref[0, pl.ds(b, 1), :]
    scale = mod_ref[1, pl.ds(b, 1), :]
    gate = mod_ref[2, pl.ds(b, 1), :]

    for j in range(rows // sub):
        r0 = j * sub
        x = x_ref[r0:r0 + sub, :]
        h = _rms_mod(x, g_ref[0:1, :], shift, scale).astype(BF16)
        z = _dot(h, w_in_ref[...])

        abuf[A_HIST_ROWS + r0:A_HIST_ROWS + r0 + sub, :] = (
            z[:, 0:D_A] * jax.nn.sigmoid(z[:, D_A:2 * D_A]))
        a_ext = abuf[r0:r0 + ext, :]
        for s in range(1, SUBLANES):
            shbuf[j % 2, s - 1] = pltpu.roll(a_ext, ext - s, axis=0)
        for c in range(D_A // LANES):
            lanes = slice(c * LANES, (c + 1) * LANES)
            for rb in range(sub // CONV_ROW_BLOCK):
                acc = jnp.broadcast_to(cab_ref[:, lanes], (CONV_ROW_BLOCK, LANES))
                for k in range(CONV_A):
                    q, s = divmod(first_a + k, SUBLANES)
                    start = q * SUBLANES + rb * CONV_ROW_BLOCK
                    if s == 0:
                        tap = abuf[r0 + start:r0 + start + CONV_ROW_BLOCK, lanes]
                    else:
                        tap = shbuf[j % 2, s - 1, start:start + CONV_ROW_BLOCK, lanes]
                    acc = acc + caw_ref[k:k + 1, lanes] * tap
                cbuf[r0 + rb * CONV_ROW_BLOCK:r0 + (rb + 1) * CONV_ROW_BLOCK, lanes] = acc
        a_out = _silu(_layernorm(cbuf[r0:r0 + sub, :], lng_ref[...], lnb_ref[...]))

        bbuf[B_HIST_ROWS + r0:B_HIST_ROWS + r0 + sub, :] = z[:, 4 * D_A:5 * D_A] * z[:, 2 * D_A:3 * D_A]
        accb = cbw_ref[0:1, :] * bbuf[pl.ds(r0 + first_b, sub), :]
        for k in range(1, CONV_B):
            accb = accb + cbw_ref[k:k + 1, :] * bbuf[pl.ds(r0 + first_b + k, sub), :]
        b_out = z[:, 3 * D_A:4 * D_A] * accb

        y = _dot(jnp.concatenate([a_out, b_out], axis=-1).astype(BF16), w_out_ref[...])
        o_ref[r0:r0 + sub, :] = x + gate * y

    @pl.when(t == pl.num_programs(1) - 1)
    def _():
        sa_ref[...] = abuf[pl.ds(rows + first_a, CONV_A - 1), :]
        sb_ref[...] = bbuf[pl.ds(rows + first_b, CONV_B - 1), :]

    abuf[0:A_HIST_ROWS, :] = abuf[rows:rows + A_HIST_ROWS, :]
    bbuf[0:B_HIST_ROWS, :] = bbuf[rows:rows + B_HIST_ROWS, :]


def _even_prompt_call(x, mod, norm_g, w_in, caw, cab, lng, lnb, cbw, w_out, l, casts):
    nb, seq, d = x.shape
    e = l // 2
    n_prompt_blk = (mod.shape[2] - nb) // nb
    n_tiles = seq // ROW_TILE
    c_in, c_out, c_shapes, c_args = _cast_specs(casts, nb, n_tiles)
    vec = lambda width: pl.BlockSpec((None, 1, width), lambda b, t: (e, 0, 0))
    outs = pl.pallas_call(
        _with_weight_casts(_even_prompt_kernel, 10, 3, len(casts)),
        grid=(nb, n_tiles),
        in_specs=[
            pl.BlockSpec((None, ROW_TILE, d), lambda b, t: (b, t, 0)),
            pl.BlockSpec((None, N_MOD, nb, d), lambda b, t: (l, 0, n_prompt_blk, 0)),
            pl.BlockSpec((None, 2, d), lambda b, t: (l, 0, 0)),
            _resident((d, D_IN_EVEN), lambda b, t: (0, 0)),
            pl.BlockSpec((None, CONV_A, D_A), lambda b, t: (e, 0, 0)),
            vec(D_A), vec(D_A), vec(D_A),
            pl.BlockSpec((None, CONV_B, D_B), lambda b, t: (e, 0, 0)),
            _resident((D_A + D_B, d), lambda b, t: (0, 0)),
        ] + c_in,
        out_specs=[
            pl.BlockSpec((None, ROW_TILE, d), lambda b, t: (b, t, 0)),
            pl.BlockSpec((None, CONV_A - 1, D_A), lambda b, t: (b, 0, 0)),
            pl.BlockSpec((None, CONV_B - 1, D_B), lambda b, t: (b, 0, 0)),
        ] + c_out,
        out_shape=[
            jax.ShapeDtypeStruct(x.shape, F32),
            jax.ShapeDtypeStruct((nb, CONV_A - 1, D_A), F32),
            jax.ShapeDtypeStruct((nb, CONV_B - 1, D_B), F32),
        ] + c_shapes,
        scratch_shapes=[
            pltpu.VMEM((ROW_TILE + A_HIST_ROWS, D_A), F32),
            pltpu.VMEM((ROW_TILE + B_HIST_ROWS, D_B), F32),
            pltpu.VMEM((2, SUBLANES - 1, CONV_SUB_ROWS + A_HIST_ROWS, D_A), F32),
            pltpu.VMEM((ROW_TILE, D_A), F32),
        ],
        compiler_params=_params(),
        name=f"even_prompt_{l}",
    )(x, mod, norm_g, w_in, caw, cab, lng, lnb, cbw, w_out, *c_args)
    return outs[0], outs[1], outs[2], outs[3:]


def _even_sample_kernel(x_ref, mod_ref, g_ref, ha_ref, hb_ref, w_in_ref, caw_ref, cab_ref,
                        lng_ref, lnb_ref, cbw_ref, w_out_ref, o_ref, sa_ref, sb_ref):
    x = x_ref[...]
    t, s, d = x.shape
    h = _rms_mod(x, g_ref[0:1, :], mod_ref[0][None], mod_ref[1][None])
    z = _dot(h.reshape(t * s, d).astype(BF16), w_in_ref[...]).reshape(t, s, D_IN_EVEN)

    a = z[:, :, 0:D_A] * jax.nn.sigmoid(z[:, :, D_A:2 * D_A])
    a_ext = jnp.concatenate([ha_ref[...], a], axis=0)
    acc = jnp.broadcast_to(cab_ref[...][None], (t, s, D_A))
    for k in range(CONV_A):
        acc = acc + caw_ref[k:k + 1, :][None] * a_ext[k:k + t]
    a_out = _silu(_layernorm(acc, lng_ref[...][None], lnb_ref[...][None]))
    sa_ref[...] = a_ext[t:]

    bx = z[:, :, 4 * D_A:5 * D_A] * z[:, :, 2 * D_A:3 * D_A]
    b_ext = jnp.concatenate([hb_ref[...], bx], axis=0)
    accb = cbw_ref[0:1, :][None] * b_ext[0:t]
    for k in range(1, CONV_B):
        accb = accb + cbw_ref[k:k + 1, :][None] * b_ext[k:k + t]
    b_out = z[:, :, 3 * D_A:4 * D_A] * accb
    sb_ref[...] = b_ext[t:]

    cat = jnp.concatenate([a_out, b_out], axis=-1).reshape(t * s, D_A + D_B).astype(BF16)
    y = _dot(cat, w_out_ref[...]).reshape(t, s, d)
    o_ref[...] = x + mod_ref[2][None] * y


def _even_sample_call(x, mod, norm_g, hist_a, hist_b, w_in, caw, cab, lng, lnb, cbw, w_out, l):
    t, ns, d = x.shape
    e = l // 2
    vec = lambda width: pl.BlockSpec((None, 1, width), lambda s, _: (e, 0, 0))
    return pl.pallas_call(
        _even_sample_kernel,
        grid=(ns // SEQ_BLOCK, 1),
        in_specs=[
            pl.BlockSpec((t, SEQ_BLOCK, d), lambda s, _: (0, s, 0)),
            pl.BlockSpec((None, N_MOD, SEQ_BLOCK, d), lambda s, _: (l, 0, s, 0)),
            pl.BlockSpec((None, 2, d), lambda s, _: (l, 0, 0)),
            pl.BlockSpec((None, CONV_A - 1, SEQ_BLOCK, D_A), lambda s, _: (e, 0, s, 0)),
            pl.BlockSpec((None, CONV_B - 1, SEQ_BLOCK, D_B), lambda s, _: (e, 0, s, 0)),
            _resident((d, D_IN_EVEN), lambda s, _: (0, 0)),
            pl.BlockSpec((None, CONV_A, D_A), lambda s, _: (e, 0, 0)),
            vec(D_A), vec(D_A), vec(D_A),
            pl.BlockSpec((None, CONV_B, D_B), lambda s, _: (e, 0, 0)),
            _resident((D_A + D_B, d), lambda s, _: (0, 0)),
        ],
        out_specs=[
            pl.BlockSpec((t, SEQ_BLOCK, d), lambda s, _: (0, s, 0)),
            pl.BlockSpec((CONV_A - 1, SEQ_BLOCK, D_A), lambda s, _: (0, s, 0)),
            pl.BlockSpec((CONV_B - 1, SEQ_BLOCK, D_B), lambda s, _: (0, s, 0)),
        ],
        out_shape=[
            jax.ShapeDtypeStruct(x.shape, F32),
            jax.ShapeDtypeStruct((CONV_A - 1, ns, D_A), F32),
            jax.ShapeDtypeStruct((CONV_B - 1, ns, D_B), F32),
        ],
        compiler_params=_params(),
        name=f"even_sample_{l}",
    )(x, mod, norm_g, hist_a, hist_b, w_in, caw, cab, lng, lnb, cbw, w_out)


def _odd_prompt_kernel(x_ref, mod_ref, g_ref, w_in_ref, b_in_ref, lng_ref, lnb_ref, ws_ref,
                       sbias_ref, w_out_ref, o_ref, cv_ref):
    b = pl.program_id(0)
    t = pl.program_id(1)
    rows = x_ref.shape[0]
    sub = ODD_SUB_ROWS
    shift = mod_ref[0, pl.ds(b, 1), :]
    scale = mod_ref[1, pl.ds(b, 1), :]
    gate = mod_ref[2, pl.ds(b, 1), :]
    causal = (lax.broadcasted_iota(jnp.int32, (CHUNK, CHUNK), 0)
              >= lax.broadcasted_iota(jnp.int32, (CHUNK, CHUNK), 1))
    ws = [jnp.where(causal, ws_ref[hd], 0.0).astype(BF16) for hd in range(C_HEADS)]

    for j in range(rows // sub):
        r0 = j * sub
        x = x_ref[r0:r0 + sub, :]
        h = _rms_mod(x, g_ref[0:1, :], shift, scale).astype(BF16)
        z = jax.nn.gelu(_dot(h, w_in_ref[...]) + b_in_ref[...])
        u = z[:, 0:D_C]
        v = _layernorm(z[:, D_C:2 * D_C], lng_ref[...], lnb_ref[...])

        if r0 + sub == rows:
            @pl.when(t == pl.num_programs(1) - 1)
            def _():
                cv_ref[...] = v[sub - CHUNK:sub, :]

        v_bf = v.astype(BF16)
        s_rows = []
        for c in range(sub // CHUNK):
            heads = [
                _dot(ws[hd], v_bf[c * CHUNK:(c + 1) * CHUNK, hd * C_HEAD_DIM:(hd + 1) * C_HEAD_DIM])
                for hd in range(C_HEADS)
            ]
            s_rows.append(jnp.concatenate(heads, axis=-1) + sbias_ref[...])
        s = jnp.concatenate(s_rows, axis=0)
        y = _dot((u * s).astype(BF16), w_out_ref[...])
        o_ref[r0:r0 + sub, :] = x + gate * y


def _odd_prompt_call(x, mod, norm_g, w_in, b_in, lng, lnb, w_s, sbias, w_out, l, casts):
    nb, seq, d = x.shape
    o = l // 2
    n_prompt_blk = (mod.shape[2] - nb) // nb
    n_tiles = seq // ROW_TILE
    c_in, c_out, c_shapes, c_args = _cast_specs(casts, nb, n_tiles)
    outs = pl.pallas_call(
        _with_weight_casts(_odd_prompt_kernel, 10, 2, len(casts)),
        grid=(nb, n_tiles),
        in_specs=[
            pl.BlockSpec((None, ROW_TILE, d), lambda b, t: (b, t, 0)),
            pl.BlockSpec((None, N_MOD, nb, d), lambda b, t: (l, 0, n_prompt_blk, 0)),
            pl.BlockSpec((None, 2, d), lambda b, t: (l, 0, 0)),
            _resident((d, 2 * D_C), lambda b, t: (0, 0)),
            pl.BlockSpec((None, 1, 2 * D_C), lambda b, t: (o, 0, 0)),
            pl.BlockSpec((None, 1, D_C), lambda b, t: (o, 0, 0)),
            pl.BlockSpec((None, 1, D_C), lambda b, t: (o, 0, 0)),
            pl.BlockSpec((None, C_HEADS, CHUNK, CHUNK), lambda b, t: (o, 0, 0, 0)),
            pl.BlockSpec((None, CHUNK, D_C), lambda b, t: (o, 0, 0)),
            _resident((D_C, d), lambda b, t: (0, 0)),
        ] + c_in,
        out_specs=[
            pl.BlockSpec((None, ROW_TILE, d), lambda b, t: (b, t, 0)),
            pl.BlockSpec((None, CHUNK, D_C), lambda b, t: (b, 0, 0)),
        ] + c_out,
        out_shape=[
            jax.ShapeDtypeStruct(x.shape, F32),
            jax.ShapeDtypeStruct((nb, CHUNK, D_C), F32),
        ] + c_shapes,
        compiler_params=_params(),
        name=f"odd_prompt_{l}",
    )(x, mod, norm_g, w_in, b_in, lng, lnb, w_s, sbias, w_out, *c_args)
    return outs[0], outs[1], outs[2:]


def _odd_sample_kernel(x_ref, mod_ref, g_ref, w_in_ref, b_in_ref, lng_ref, lnb_ref, wm_ref,
                       sbias_ref, w_out_ref, o_ref, cv_ref):
    x = x_ref[...]
    t, s, d = x.shape
    h = _rms_mod(x, g_ref[0:1, :], mod_ref[0][None], mod_ref[1][None])
    z = jax.nn.gelu(_dot(h.reshape(t * s, d).astype(BF16), w_in_ref[...]) + b_in_ref[...])
    z = z.reshape(t, s, 2 * D_C)
    u = z[:, :, 0:D_C]
    v = _layernorm(z[:, :, D_C:2 * D_C], lng_ref[...][None], lnb_ref[...][None])
    cv_ref[...] = v
    gated = []
    for i in range(t):
        s_i = jnp.broadcast_to(sbias_ref[i:i + 1, :], (s, D_C))
        for j in range(i + 1):
            s_i = s_i + wm_ref[i, j:j + 1, :] * v[j]
        gated.append(u[i] * s_i)
    us = jnp.stack(gated, axis=0).reshape(t * s, D_C).astype(BF16)
    y = _dot(us, w_out_ref[...]).reshape(t, s, d)
    o_ref[...] = x + mod_ref[2][None] * y


def _odd_sample_call(x, mod, norm_g, w_in, b_in, lng, lnb, wm, sbias, w_out, l):
    t, ns, d = x.shape
    o = l // 2
    return pl.pallas_call(
        _odd_sample_kernel,
        grid=(ns // SEQ_BLOCK, 1),
        in_specs=[
            pl.BlockSpec((t, SEQ_BLOCK, d), lambda s, _: (0, s, 0)),
            pl.BlockSpec((None, N_MOD, SEQ_BLOCK, d), lambda s, _: (l, 0, s, 0)),
            pl.BlockSpec((None, 2, d), lambda s, _: (l, 0, 0)),
            _resident((d, 2 * D_C), lambda s, _: (0, 0)),
            pl.BlockSpec((None, 1, 2 * D_C), lambda s, _: (o, 0, 0)),
            pl.BlockSpec((None, 1, D_C), lambda s, _: (o, 0, 0)),
            pl.BlockSpec((None, 1, D_C), lambda s, _: (o, 0, 0)),
            pl.BlockSpec((None, t, t, D_C), lambda s, _: (o, 0, 0, 0)),
            pl.BlockSpec((None, t, D_C), lambda s, _: (o, 0, 0)),
            _resident((D_C, d), lambda s, _: (0, 0)),
        ],
        out_specs=[
            pl.BlockSpec((t, SEQ_BLOCK, d), lambda s, _: (0, s, 0)),
            pl.BlockSpec((t, SEQ_BLOCK, D_C), lambda s, _: (0, s, 0)),
        ],
        out_shape=[
            jax.ShapeDtypeStruct(x.shape, F32),
            jax.ShapeDtypeStruct((t, ns, D_C), F32),
        ],
        compiler_params=_params(),
        name=f"odd_sample_{l}",
    )(x, mod, norm_g, w_in, b_in, lng, lnb, wm, sbias, w_out)


def kernel(x_prompt, x_sample, state_conv_a, state_conv_b, c_prompt, c_sample, w_in_ab, conv_a_w, conv_a_b, ln_a_g, ln_a_b, conv_b_w, w_out_ab, w_in_c, b_in_c, ln_v_g, ln_v_b, w_s, b_s, w_out_c, w_ada, b_ada, norm_g, w_ff1, w_ff2, final_g):
    dec_seq = x_sample.shape[1]
    n_even, n_odd = w_in_ab.shape[0], w_in_c.shape[0]

    mix_in_bf = w_in_ab[0].astype(BF16)
    mix_out_bf = w_out_ab[0].astype(BF16)

    cab3 = conv_a_b.reshape(n_even, 1, D_A)
    lnag3 = ln_a_g.reshape(n_even, 1, D_A)
    lnab3 = ln_a_b.reshape(n_even, 1, D_A)
    binc3 = b_in_c.reshape(n_odd, 1, 2 * D_C)
    lnvg3 = ln_v_g.reshape(n_odd, 1, D_C)
    lnvb3 = ln_v_b.reshape(n_odd, 1, D_C)
    fg2 = final_g.reshape(1, D_MODEL)

    sbias = jnp.repeat(jnp.swapaxes(b_s, 1, 2), C_HEAD_DIM, axis=2)
    wm = jnp.repeat(jnp.transpose(w_s[:, :, :dec_seq, :dec_seq], (0, 2, 3, 1)),
                    C_HEAD_DIM, axis=3)

    mod = _ada_call(jnp.concatenate([c_sample, c_prompt], axis=0), w_ada, b_ada)

    xp = x_prompt
    xs = jnp.transpose(x_sample, (1, 0, 2))
    ha = jnp.transpose(state_conv_a, (0, 2, 1, 3))
    hb = jnp.transpose(state_conv_b, (0, 2, 1, 3))

    a_p, b_p, v_p, a_s, b_s_out, v_s = [], [], [], [], [], []
    for l in range(DEPTH):
        final = l == DEPTH - 1
        ff_casts = [(w_ff1, l), (w_ff2, l)]
        if l % 2 == 0:
            xp, sa, sb, (ff1_bf, ff2_bf) = _even_prompt_call(
                xp, mod, norm_g, mix_in_bf, conv_a_w, cab3, lnag3, lnab3, conv_b_w, mix_out_bf, l, ff_casts)
            a_p.append(sa)
            b_p.append(sb)
            xs, sa, sb = _even_sample_call(xs, mod, norm_g, ha, hb, mix_in_bf, conv_a_w, cab3, lnag3,
                                           lnab3, conv_b_w, mix_out_bf, l)
            a_s.append(jnp.transpose(sa, (1, 0, 2)))
            b_s_out.append(jnp.transpose(sb, (1, 0, 2)))
        else:
            xp, cv, (ff1_bf, ff2_bf) = _odd_prompt_call(
                xp, mod, norm_g, mix_in_bf, binc3, lnvg3, lnvb3, w_s, sbias, mix_out_bf, l, ff_casts)
            v_p.append(cv)
            xs, cv = _odd_sample_call(xs, mod, norm_g, mix_in_bf, binc3, lnvg3, lnvb3, wm,
                                      sbias[:, :dec_seq], mix_out_bf, l)
            v_s.append(jnp.transpose(cv, (1, 0, 2)))
        if final:
            mix_casts = []
        elif l % 2 == 0:
            mix_casts = [(w_in_c, l // 2), (w_out_c, l // 2)]
        else:
            mix_casts = [(w_in_ab, (l + 1) // 2), (w_out_ab, (l + 1) // 2)]
        xp, xs, next_mix = _mlp_call(xp, xs, mod, norm_g, ff1_bf, ff2_bf, fg2, l, final, mix_casts)
        if next_mix:
            mix_in_bf, mix_out_bf = next_mix

    return (xp, jnp.transpose(xs, (1, 0, 2)), jnp.stack(a_p), jnp.stack(a_s), jnp.stack(b_p),
            jnp.stack(b_s_out), jnp.stack(v_p), jnp.stack(v_s))
```

```python
import functools

import jax
import jax.numpy as jnp
from jax import lax
from jax.experimental import pallas as pl
from jax.experimental.pallas import tpu as pltpu

D_MODEL = 1024
DEPTH = 4
D_A = 512
D_B = 512
CONV_A = 31
CONV_B = 3
D_IN_EVEN = 2 * D_A + 3 * D_B
D_C = 1024
C_HEADS = 8
C_HEAD_DIM = 128
CHUNK = 128
D_FF = 4096
N_MOD = 6
EPS = 1e-6

BF16 = jnp.bfloat16
F32 = jnp.float32

VMEM_LIMIT_BYTES = 56 * 1024 * 1024
SUBLANES = 8
ROW_TILE = 1024
ODD_SUB_ROWS = 512
MLP_ROW_TILE = 512
SEQ_BLOCK = 64
FF_CHUNK = 1024
ADA_GROUP = 3
A_HIST_ROWS = 32
B_HIST_ROWS = 8
LANES = 128
CONV_SUB_ROWS = 256
CONV_ROW_BLOCK = 128


def _params():
    return pltpu.CompilerParams(
        dimension_semantics=("arbitrary", "arbitrary"),
        vmem_limit_bytes=VMEM_LIMIT_BYTES)


def _resident(shape, index_map):
    return pl.BlockSpec(shape, index_map, pipeline_mode=pl.Buffered(1))


def _with_weight_casts(body, n_in, n_out, n_cast):
    def wrapped(*refs):
        ins = refs[:n_in]
        cast_src = refs[n_in:n_in + n_cast]
        outs = refs[n_in + n_cast:n_in + n_cast + n_out]
        cast_dst = refs[n_in + n_cast + n_out:n_in + 2 * n_cast + n_out]
        scratch = refs[n_in + 2 * n_cast + n_out:]
        for src, dst in zip(cast_src, cast_dst):
            dst[...] = src[...].astype(BF16)
        body(*ins, *outs, *scratch)
    return wrapped


def _cast_specs(casts, n_outer, n_inner):
    in_specs, out_specs, out_shapes, args = [], [], [], []
    for w, layer in casts:
        _, r, c = w.shape
        rb = r // (n_outer * n_inner)
        in_specs.append(pl.BlockSpec((None, rb, c), lambda b, t, layer=layer: (layer, b * n_inner + t, 0)))
        out_specs.append(pl.BlockSpec((rb, c), lambda b, t: (b * n_inner + t, 0)))
        out_shapes.append(jax.ShapeDtypeStruct((r, c), BF16))
        args.append(w)
    return in_specs, out_specs, out_shapes, args


def _rms(x, g):
    ms = jnp.mean(x * x, axis=-1, keepdims=True)
    return x * lax.rsqrt(ms + EPS) * g


def _rms_mod(x, g, shift, scale):
    ms = jnp.mean(x * x, axis=-1, keepdims=True)
    return x * lax.rsqrt(ms + EPS) * (g * (1.0 + scale)) + shift


def _layernorm(x, g, b):
    mu = jnp.mean(x, axis=-1, keepdims=True)
    xc = x - mu
    var = jnp.mean(xc * xc, axis=-1, keepdims=True)
    return xc * lax.rsqrt(var + EPS) * g + b


def _silu(x):
    return x * jax.nn.sigmoid(x)


def _dot(a, b):
    return jnp.dot(a, b, preferred_element_type=F32)


def _ada_kernel(c_ref, w_ref, b_ref, o_ref):
    c = c_ref[...]
    res = _dot(_silu(c).astype(BF16), w_ref[...].astype(BF16))
    for i in range(ADA_GROUP):
        o_ref[i] = res[:, i * D_MODEL:(i + 1) * D_MODEL] + b_ref[i]


def _ada_call(c_all, w_ada, b_ada):
    n = c_all.shape[0]
    b4 = b_ada.reshape(DEPTH, N_MOD, 1, D_MODEL)
    return pl.pallas_call(
        _ada_kernel,
        grid=(DEPTH, N_MOD // ADA_GROUP),
        in_specs=[
            pl.BlockSpec((n, D_MODEL), lambda l, j: (0, 0)),
            pl.BlockSpec((None, D_MODEL, ADA_GROUP * D_MODEL), lambda l, j: (l, 0, j)),
            pl.BlockSpec((None, ADA_GROUP, 1, D_MODEL), lambda l, j: (l, j, 0, 0)),
        ],
        out_specs=pl.BlockSpec((None, ADA_GROUP, n, D_MODEL), lambda l, j: (l, j, 0, 0)),
        out_shape=jax.ShapeDtypeStruct((DEPTH, N_MOD, n, D_MODEL), F32),
        compiler_params=_params(),
        name="adaln",
    )(c_all, w_ada, b4)


def _mlp_body(x2d, h_bf, gate, w1_ref, w2_ref):
    acc = None
    for j in range(D_FF // FF_CHUNK):
        cols = slice(j * FF_CHUNK, (j + 1) * FF_CHUNK)
        hid = _dot(h_bf, w1_ref[:, cols])
        hid = jnp.square(jnp.maximum(hid, 0.0)).astype(BF16)
        part = _dot(hid, w2_ref[cols, :])
        acc = part if acc is None else acc + part
    return x2d + gate * acc


def _mlp_prompt_kernel(x_ref, mod_ref, g_ref, w1_ref, w2_ref, fg_ref, xs_ref, mods_ref, o_ref, os_ref,
                       *, final):
    b = pl.program_id(0)
    x = x_ref[...]
    shift = mod_ref[3, pl.ds(b, 1), :]
    scale = mod_ref[4, pl.ds(b, 1), :]
    gate = mod_ref[5, pl.ds(b, 1), :]
    h = _rms_mod(x, g_ref[1:2, :], shift, scale).astype(BF16)
    y = _mlp_body(x, h, gate, w1_ref, w2_ref)
    if final:
        y = _rms(y, fg_ref[...])
    o_ref[...] = y

    @pl.when(jnp.logical_and(b == pl.num_programs(0) - 1,
                             pl.program_id(1) == pl.num_programs(1) - 1))
    def _():
        t, s, d = xs_ref.shape
        half = t // 2
        for p in range(0, t, half):
            xs = xs_ref[p:p + half]
            hs = _rms_mod(xs, g_ref[1:2, :], mods_ref[3][None], mods_ref[4][None])
            hs = hs.reshape(half * s, d).astype(BF16)
            gs = jnp.broadcast_to(mods_ref[5][None], (half, s, d)).reshape(half * s, d)
            ys = _mlp_body(xs.reshape(half * s, d), hs, gs, w1_ref, w2_ref)
            if final:
                ys = _rms(ys, fg_ref[...])
            os_ref[p:p + half] = ys.reshape(half, s, d)


def _mlp_call(x, xs, mod, norm_g, w1, w2, final_g, l, final, casts):
    nb, seq, d = x.shape
    ts, ns, _ = xs.shape
    n_prompt_blk = (mod.shape[2] - nb) // nb
    n_tiles = seq // MLP_ROW_TILE
    c_in, c_out, c_shapes, c_args = _cast_specs(casts, nb, n_tiles)
    outs = pl.pallas_call(
        _with_weight_casts(functools.partial(_mlp_prompt_kernel, final=final), 8, 2, len(casts)),
        grid=(nb, n_tiles),
        in_specs=[
            pl.BlockSpec((None, MLP_ROW_TILE, d), lambda b, t: (b, t, 0)),
            pl.BlockSpec((None, N_MOD, nb, d), lambda b, t: (l, 0, n_prompt_blk, 0)),
            pl.BlockSpec((None, 2, d), lambda b, t: (l, 0, 0)),
            _resident((d, D_FF), lambda b, t: (0, 0)),
            _resident((D_FF, d), lambda b, t: (0, 0)),
            pl.BlockSpec((1, d), lambda b, t: (0, 0)),
            _resident((ts, ns, d), lambda b, t: (0, 0, 0)),
            _resident((None, N_MOD, ns, d), lambda b, t: (l, 0, 0, 0)),
        ] + c_in,
        out_specs=[pl.BlockSpec((None, MLP_ROW_TILE, d), lambda b, t: (b, t, 0)),
                   pl.BlockSpec((ts, ns, d), lambda b, t: (0, 0, 0))] + c_out,
        out_shape=[jax.ShapeDtypeStruct(x.shape, F32), jax.ShapeDtypeStruct(xs.shape, F32)] + c_shapes,
        compiler_params=_params(),
        name=f"mlp_{l}",
    )(x, mod, norm_g, w1, w2, final_g, xs, mod, *c_args)
    return outs[0], outs[1], outs[2:]


def _even_prompt_kernel(x_ref, mod_ref, g_ref, w_in_ref, caw_ref, cab_ref, lng_ref, lnb_ref,
                        cbw_ref, w_out_ref, o_ref, sa_ref, sb_ref, abuf, bbuf, shbuf, cbuf):
    b = pl.program_id(0)
    t = pl.program_id(1)
    rows = x_ref.shape[0]
    sub = CONV_SUB_ROWS
    ext = sub + A_HIST_ROWS
    first_a = A_HIST_ROWS - (CONV_A - 1)
    first_b = B_HIST_ROWS - (CONV_B - 1)

    @pl.when(t == 0)
    def _():
        abuf[0:A_HIST_ROWS, :] = jnp.zeros((A_HIST_ROWS, D_A), F32)
        bbuf[0:B_HIST_ROWS, :] = jnp.zeros((B_HIST_ROWS, D_B), F32)

    shift = mod_ref[0, pl.ds(b, 1), :]
    scale = mod_ref[1, pl.ds(b, 1), :]
    gate = mod_ref[2, pl.ds(b, 1), :]

    for j in range(rows // sub):
        r0 = j * sub
        x = x_ref[r0:r0 + sub, :]
        h = _rms_mod(x, g_ref[0:1, :], shift, scale).astype(BF16)
        z = _dot(h, w_in_ref[...])

        abuf[A_HIST_ROWS + r0:A_HIST_ROWS + r0 + sub, :] = (
            z[:, 0:D_A] * jax.nn.sigmoid(z[:, D_A:2 * D_A]))
        a_ext = abuf[r0:r0 + ext, :]
        for s in range(1, SUBLANES):
            shbuf[j % 2, s - 1] = pltpu.roll(a_ext, ext - s, axis=0)
        for c in range(D_A // LANES):
            lanes = slice(c * LANES, (c + 1) * LANES)
            for rb in range(sub // CONV_ROW_BLOCK):
                acc = jnp.broadcast_to(cab_ref[:, lanes], (CONV_ROW_BLOCK, LANES))
                for k in range(CONV_A):
                    q, s = divmod(first_a + k, SUBLANES)
                    start = q * SUBLANES + rb * CONV_ROW_BLOCK
                    if s == 0:
                        tap = abuf[r0 + start:r0 + start + CONV_ROW_BLOCK, lanes]
                    else:
                        tap = shbuf[j % 2, s - 1, start:start + CONV_ROW_BLOCK, lanes]
                    acc = acc + caw_ref[k:k + 1, lanes] * tap
                cbuf[r0 + rb * CONV_ROW_BLOCK:r0 + (rb + 1) * CONV_ROW_BLOCK, lanes] = acc
        a_out = _silu(_layernorm(cbuf[r0:r0 + sub, :], lng_ref[...], lnb_ref[...]))

        bbuf[B_HIST_ROWS + r0:B_HIST_ROWS + r0 + sub, :] = z[:, 4 * D_A:5 * D_A] * z[:, 2 * D_A:3 * D_A]
        accb = cbw_ref[0:1, :] * bbuf[pl.ds(r0 + first_b, sub), :]
        for k in range(1, CONV_B):
            accb = accb + cbw_ref[k:k + 1, :] * bbuf[pl.ds(r0 + first_b + k, sub), :]
        b_out = z[:, 3 * D_A:4 * D_A] * accb

        y = _dot(jnp.concatenate([a_out, b_out], axis=-1).astype(BF16), w_out_ref[...])
        o_ref[r0:r0 + sub, :] = x + gate * y

    @pl.when(t == pl.num_programs(1) - 1)
    def _():
        sa_ref[...] = abuf[pl.ds(rows + first_a, CONV_A - 1), :]
        sb_ref[...] = bbuf[pl.ds(rows + first_b, CONV_B - 1), :]

    abuf[0:A_HIST_ROWS, :] = abuf[rows:rows + A_HIST_ROWS, :]
    bbuf[0:B_HIST_ROWS, :] = bbuf[rows:rows + B_HIST_ROWS, :]


def _even_prompt_call(x, mod, norm_g, w_in, caw, cab, lng, lnb, cbw, w_out, l, casts):
    nb, seq, d = x.shape
    e = l // 2
    n_prompt_blk = (mod.shape[2] - nb) // nb
    n_tiles = seq // ROW_TILE
    c_in, c_out, c_shapes, c_args = _cast_specs(casts, nb, n_tiles)
    vec = lambda width: pl.BlockSpec((None, 1, width), lambda b, t: (e, 0, 0))
    outs = pl.pallas_call(
        _with_weight_casts(_even_prompt_kernel, 10, 3, len(casts)),
        grid=(nb, n_tiles),
        in_specs=[
            pl.BlockSpec((None, ROW_TILE, d), lambda b, t: (b, t, 0)),
            pl.BlockSpec((None, N_MOD, nb, d), lambda b, t: (l, 0, n_prompt_blk, 0)),
            pl.BlockSpec((None, 2, d), lambda b, t: (l, 0, 0)),
            _resident((d, D_IN_EVEN), lambda b, t: (0, 0)),
            pl.BlockSpec((None, CONV_A, D_A), lambda b, t: (e, 0, 0)),
            vec(D_A), vec(D_A), vec(D_A),
            pl.BlockSpec((None, CONV_B, D_B), lambda b, t: (e, 0, 0)),
            _resident((D_A + D_B, d), lambda b, t: (0, 0)),
        ] + c_in,
        out_specs=[
            pl.BlockSpec((None, ROW_TILE, d), lambda b, t: (b, t, 0)),
            pl.BlockSpec((None, CONV_A - 1, D_A), lambda b, t: (b, 0, 0)),
            pl.BlockSpec((None, CONV_B - 1, D_B), lambda b, t: (b, 0, 0)),
        ] + c_out,
        out_shape=[
            jax.ShapeDtypeStruct(x.shape, F32),
            jax.ShapeDtypeStruct((nb, CONV_A - 1, D_A), F32),
            jax.ShapeDtypeStruct((nb, CONV_B - 1, D_B), F32),
        ] + c_shapes,
        scratch_shapes=[
            pltpu.VMEM((ROW_TILE + A_HIST_ROWS, D_A), F32),
            pltpu.VMEM((ROW_TILE + B_HIST_ROWS, D_B), F32),
            pltpu.VMEM((2, SUBLANES - 1, CONV_SUB_ROWS + A_HIST_ROWS, D_A), F32),
            pltpu.VMEM((ROW_TILE, D_A), F32),
        ],
        compiler_params=_params(),
        name=f"even_prompt_{l}",
    )(x, mod, norm_g, w_in, caw, cab, lng, lnb, cbw, w_out, *c_args)
    return outs[0], outs[1], outs[2], outs[3:]


def _even_sample_kernel(x_ref, mod_ref, g_ref, ha_ref, hb_ref, w_in_ref, caw_ref, cab_ref,
                        lng_ref, lnb_ref, cbw_ref, w_out_ref, o_ref, sa_ref, sb_ref):
    x = x_ref[...]
    t, s, d = x.shape
    h = _rms_mod(x, g_ref[0:1, :], mod_ref[0][None], mod_ref[1][None])
    z = _dot(h.reshape(t * s, d).astype(BF16), w_in_ref[...]).reshape(t, s, D_IN_EVEN)

    a = z[:, :, 0:D_A] * jax.nn.sigmoid(z[:, :, D_A:2 * D_A])
    a_ext = jnp.concatenate([ha_ref[...], a], axis=0)
    acc = jnp.broadcast_to(cab_ref[...][None], (t, s, D_A))
    for k in range(CONV_A):
        acc = acc + caw_ref[k:k + 1, :][None] * a_ext[k:k + t]
    a_out = _silu(_layernorm(acc, lng_ref[...][None], lnb_ref[...][None]))
    sa_ref[...] = a_ext[t:]

    bx = z[:, :, 4 * D_A:5 * D_A] * z[:, :, 2 * D_A:3 * D_A]
    b_ext = jnp.concatenate([hb_ref[...], bx], axis=0)
    accb = cbw_ref[0:1, :][None] * b_ext[0:t]
    for k in range(1, CONV_B):
        accb = accb + cbw_ref[k:k + 1, :][None] * b_ext[k:k + t]
    b_out = z[:, :, 3 * D_A:4 * D_A] * accb
    sb_ref[...] = b_ext[t:]

    cat = jnp.concatenate([a_out, b_out], axis=-1).reshape(t * s, D_A + D_B).astype(BF16)
    y = _dot(cat, w_out_ref[...]).reshape(t, s, d)
    o_ref[...] = x + mod_ref[2][None] * y


def _even_sample_call(x, mod, norm_g, hist_a, hist_b, w_in, caw, cab, lng, lnb, cbw, w_out, l):
    t, ns, d = x.shape
    e = l // 2
    vec = lambda width: pl.BlockSpec((None, 1, width), lambda s, _: (e, 0, 0))
    return pl.pallas_call(
        _even_sample_kernel,
        grid=(ns // SEQ_BLOCK, 1),
        in_specs=[
            pl.BlockSpec((t, SEQ_BLOCK, d), lambda s, _: (0, s, 0)),
            pl.BlockSpec((None, N_MOD, SEQ_BLOCK, d), lambda s, _: (l, 0, s, 0)),
            pl.BlockSpec((None, 2, d), lambda s, _: (l, 0, 0)),
            pl.BlockSpec((None, CONV_A - 1, SEQ_BLOCK, D_A), lambda s, _: (e, 0, s, 0)),
            pl.BlockSpec((None, CONV_B - 1, SEQ_BLOCK, D_B), lambda s, _: (e, 0, s, 0)),
            _resident((d, D_IN_EVEN), lambda s, _: (0, 0)),
            pl.BlockSpec((None, CONV_A, D_A), lambda s, _: (e, 0, 0)),
            vec(D_A), vec(D_A), vec(D_A),
            pl.BlockSpec((None, CONV_B, D_B), lambda s, _: (e, 0, 0)),
            _resident((D_A + D_B, d), lambda s, _: (0, 0)),
        ],
        out_specs=[
            pl.BlockSpec((t, SEQ_BLOCK, d), lambda s, _: (0, s, 0)),
            pl.BlockSpec((CONV_A - 1, SEQ_BLOCK, D_A), lambda s, _: (0, s, 0)),
            pl.BlockSpec((CONV_B - 1, SEQ_BLOCK, D_B), lambda s, _: (0, s, 0)),
        ],
        out_shape=[
            jax.ShapeDtypeStruct(x.shape, F32),
            jax.ShapeDtypeStruct((CONV_A - 1, ns, D_A), F32),
            jax.ShapeDtypeStruct((CONV_B - 1, ns, D_B), F32),
        ],
        compiler_params=_params(),
        name=f"even_sample_{l}",
    )(x, mod, norm_g, hist_a, hist_b, w_in, caw, cab, lng, lnb, cbw, w_out)


def _odd_prompt_kernel(x_ref, mod_ref, g_ref, w_in_ref, b_in_ref, lng_ref, lnb_ref, ws_ref,
                       sbias_ref, w_out_ref, xs_ref, mods_ref, wm_ref, o_ref, cv_ref, os_ref, cvs_ref):
    b = pl.program_id(0)
    t = pl.program_id(1)

    @pl.when(b == pl.num_programs(0) - 1)
    def _():
        _odd_sample_body(xs_ref, mods_ref, g_ref, w_in_ref, b_in_ref, lng_ref, lnb_ref, wm_ref,
                         sbias_ref, w_out_ref, os_ref, cvs_ref)

    rows = x_ref.shape[0]
    sub = ODD_SUB_ROWS
    shift = mod_ref[0, pl.ds(b, 1), :]
    scale = mod_ref[1, pl.ds(b, 1), :]
    gate = mod_ref[2, pl.ds(b, 1), :]
    causal = (lax.broadcasted_iota(jnp.int32, (CHUNK, CHUNK), 0)
              >= lax.broadcasted_iota(jnp.int32, (CHUNK, CHUNK), 1))
    ws = [jnp.where(causal, ws_ref[hd], 0.0).astype(BF16) for hd in range(C_HEADS)]

    for j in range(rows // sub):
        r0 = j * sub
        x = x_ref[r0:r0 + sub, :]
        h = _rms_mod(x, g_ref[0:1, :], shift, scale).astype(BF16)
        z = jax.nn.gelu(_dot(h, w_in_ref[...]) + b_in_ref[...])
        u = z[:, 0:D_C]
        v = _layernorm(z[:, D_C:2 * D_C], lng_ref[...], lnb_ref[...])

        if r0 + sub == rows:
            @pl.when(t == pl.num_programs(1) - 1)
            def _():
                cv_ref[...] = v[sub - CHUNK:sub, :]

        v_bf = v.astype(BF16)
        s_rows = []
        for c in range(sub // CHUNK):
            heads = [
                _dot(ws[hd], v_bf[c * CHUNK:(c + 1) * CHUNK, hd * C_HEAD_DIM:(hd + 1) * C_HEAD_DIM])
                for hd in range(C_HEADS)
            ]
            s_rows.append(jnp.concatenate(heads, axis=-1) + sbias_ref[...])
        s = jnp.concatenate(s_rows, axis=0)
        y = _dot((u * s).astype(BF16), w_out_ref[...])
        o_ref[r0:r0 + sub, :] = x + gate * y


def _odd_call(x, xs, mod, norm_g, w_in, b_in, lng, lnb, w_s, sbias, wm, w_out, l, casts):
    nb, seq, d = x.shape
    ts, ns, _ = xs.shape
    o = l // 2
    n_prompt_blk = (mod.shape[2] - nb) // nb
    n_tiles = seq // ROW_TILE
    seq_blk = ns // n_tiles
    sblk = lambda b, t: jnp.where(b == nb - 1, t, 0)
    c_in, c_out, c_shapes, c_args = _cast_specs(casts, nb, n_tiles)
    outs = pl.pallas_call(
        _with_weight_casts(_odd_prompt_kernel, 13, 4, len(casts)),
        grid=(nb, n_tiles),
        in_specs=[
            pl.BlockSpec((None, ROW_TILE, d), lambda b, t: (b, t, 0)),
            pl.BlockSpec((None, N_MOD, nb, d), lambda b, t: (l, 0, n_prompt_blk, 0)),
            pl.BlockSpec((None, 2, d), lambda b, t: (l, 0, 0)),
            _resident((d, 2 * D_C), lambda b, t: (0, 0)),
            pl.BlockSpec((None, 1, 2 * D_C), lambda b, t: (o, 0, 0)),
            pl.BlockSpec((None, 1, D_C), lambda b, t: (o, 0, 0)),
            pl.BlockSpec((None, 1, D_C), lambda b, t: (o, 0, 0)),
            pl.BlockSpec((None, C_HEADS, CHUNK, CHUNK), lambda b, t: (o, 0, 0, 0)),
            pl.BlockSpec((None, CHUNK, D_C), lambda b, t: (o, 0, 0)),
            _resident((D_C, d), lambda b, t: (0, 0)),
            pl.BlockSpec((ts, seq_blk, d), lambda b, t: (0, sblk(b, t), 0)),
            pl.BlockSpec((None, N_MOD // 2, seq_blk, d), lambda b, t: (l, 0, sblk(b, t), 0)),
            pl.BlockSpec((None, ts, ts, D_C), lambda b, t: (o, 0, 0, 0)),
        ] + c_in,
        out_specs=[
            pl.BlockSpec((None, ROW_TILE, d), lambda b, t: (b, t, 0)),
            pl.BlockSpec((None, CHUNK, D_C), lambda b, t: (b, 0, 0)),
            pl.BlockSpec((ts, seq_blk, d), lambda b, t: (0, sblk(b, t), 0)),
            pl.BlockSpec((ts, seq_blk, D_C), lambda b, t: (0, sblk(b, t), 0)),
        ] + c_out,
        out_shape=[
            jax.ShapeDtypeStruct(x.shape, F32),
            jax.ShapeDtypeStruct((nb, CHUNK, D_C), F32),
            jax.ShapeDtypeStruct(xs.shape, F32),
            jax.ShapeDtypeStruct((ts, ns, D_C), F32),
        ] + c_shapes,
        compiler_params=_params(),
        name=f"odd_{l}",
    )(x, mod, norm_g, w_in, b_in, lng, lnb, w_s, sbias, w_out, xs, mod, wm, *c_args)
    return outs[0], outs[1], outs[2], outs[3], outs[4:]


def _odd_sample_body(x_ref, mod_ref, g_ref, w_in_ref, b_in_ref, lng_ref, lnb_ref, wm_ref,
                     sbias_ref, w_out_ref, o_ref, cv_ref):
    x = x_ref[...]
    t, s, d = x.shape
    h = _rms_mod(x, g_ref[0:1, :], mod_ref[0][None], mod_ref[1][None])
    z = jax.nn.gelu(_dot(h.reshape(t * s, d).astype(BF16), w_in_ref[...]) + b_in_ref[...])
    z = z.reshape(t, s, 2 * D_C)
    u = z[:, :, 0:D_C]
    v = _layernorm(z[:, :, D_C:2 * D_C], lng_ref[...][None], lnb_ref[...][None])
    cv_ref[...] = v
    gated = []
    for i in range(t):
        s_i = jnp.broadcast_to(sbias_ref[i:i + 1, :], (s, D_C))
        for j in range(i + 1):
            s_i = s_i + wm_ref[i, j:j + 1, :] * v[j]
        gated.append(u[i] * s_i)
    us = jnp.stack(gated, axis=0).reshape(t * s, D_C).astype(BF16)
    y = _dot(us, w_out_ref[...]).reshape(t, s, d)
    o_ref[...] = x + mod_ref[2][None] * y


def kernel(x_prompt, x_sample, state_conv_a, state_conv_b, c_prompt, c_sample, w_in_ab, conv_a_w, conv_a_b, ln_a_g, ln_a_b, conv_b_w, w_out_ab, w_in_c, b_in_c, ln_v_g, ln_v_b, w_s, b_s, w_out_c, w_ada, b_ada, norm_g, w_ff1, w_ff2, final_g):
    dec_seq = x_sample.shape[1]
    n_even, n_odd = w_in_ab.shape[0], w_in_c.shape[0]

    mix_in_bf = w_in_ab[0].astype(BF16)
    mix_out_bf = w_out_ab[0].astype(BF16)

    cab3 = conv_a_b.reshape(n_even, 1, D_A)
    lnag3 = ln_a_g.reshape(n_even, 1, D_A)
    lnab3 = ln_a_b.reshape(n_even, 1, D_A)
    binc3 = b_in_c.reshape(n_odd, 1, 2 * D_C)
    lnvg3 = ln_v_g.reshape(n_odd, 1, D_C)
    lnvb3 = ln_v_b.reshape(n_odd, 1, D_C)
    fg2 = final_g.reshape(1, D_MODEL)

    sbias = jnp.repeat(jnp.swapaxes(b_s, 1, 2), C_HEAD_DIM, axis=2)
    wm = jnp.repeat(jnp.transpose(w_s[:, :, :dec_seq, :dec_seq], (0, 2, 3, 1)),
                    C_HEAD_DIM, axis=3)

    mod = _ada_call(jnp.concatenate([c_sample, c_prompt], axis=0), w_ada, b_ada)

    xp = x_prompt
    xs = jnp.transpose(x_sample, (1, 0, 2))
    ha = jnp.transpose(state_conv_a, (0, 2, 1, 3))
    hb = jnp.transpose(state_conv_b, (0, 2, 1, 3))

    a_p, b_p, v_p, a_s, b_s_out, v_s = [], [], [], [], [], []
    for l in range(DEPTH):
        final = l == DEPTH - 1
        ff_casts = [(w_ff1, l), (w_ff2, l)]
        if l % 2 == 0:
            xp, sa, sb, (ff1_bf, ff2_bf) = _even_prompt_call(
                xp, mod, norm_g, mix_in_bf, conv_a_w, cab3, lnag3, lnab3, conv_b_w, mix_out_bf, l, ff_casts)
            a_p.append(sa)
            b_p.append(sb)
            xs, sa, sb = _even_sample_call(xs, mod, norm_g, ha, hb, mix_in_bf, conv_a_w, cab3, lnag3,
                                           lnab3, conv_b_w, mix_out_bf, l)
            a_s.append(jnp.transpose(sa, (1, 0, 2)))
            b_s_out.append(jnp.transpose(sb, (1, 0, 2)))
        else:
            xp, cv, xs, cvs, (ff1_bf, ff2_bf) = _odd_call(
                xp, xs, mod, norm_g, mix_in_bf, binc3, lnvg3, lnvb3, w_s, sbias, wm, mix_out_bf, l, ff_casts)
            v_p.append(cv)
            v_s.append(jnp.transpose(cvs, (1, 0, 2)))
        if final:
            mix_casts = []
        elif l % 2 == 0:
            mix_casts = [(w_in_c, l // 2), (w_out_c, l // 2)]
        else:
            mix_casts = [(w_in_ab, (l + 1) // 2), (w_out_ab, (l + 1) // 2)]
        xp, xs, next_mix = _mlp_call(xp, xs, mod, norm_g, ff1_bf, ff2_bf, fg2, l, final, mix_casts)
        if next_mix:
            mix_in_bf, mix_out_bf = next_mix

    return (xp, jnp.transpose(xs, (1, 0, 2)), jnp.stack(a_p), jnp.stack(a_s), jnp.stack(b_p),
            jnp.stack(b_s_out), jnp.stack(v_p), jnp.stack(v_s))
```

```python
import functools

import jax
import jax.numpy as jnp
from jax import lax
from jax.experimental import pallas as pl
from jax.experimental.pallas import tpu as pltpu

D_MODEL = 1024
DEPTH = 4
D_A = 512
D_B = 512
CONV_A = 31
CONV_B = 3
D_IN_EVEN = 2 * D_A + 3 * D_B
D_C = 1024
C_HEADS = 8
C_HEAD_DIM = 128
CHUNK = 128
D_FF = 4096
N_MOD = 6
EPS = 1e-6

BF16 = jnp.bfloat16
F32 = jnp.float32

VMEM_LIMIT_BYTES = 56 * 1024 * 1024
SUBLANES = 8
ROW_TILE = 1024
ODD_SUB_ROWS = 512
MLP_ROW_TILE = 512
SEQ_BLOCK = 64
FF_CHUNK = 1024
ADA_GROUP = 3
A_HIST_ROWS = 32
B_HIST_ROWS = 8
LANES = 128
CONV_SUB_ROWS = 256
CONV_ROW_BLOCK = 128


def _params():
    return pltpu.CompilerParams(
        dimension_semantics=("arbitrary", "arbitrary"),
        vmem_limit_bytes=VMEM_LIMIT_BYTES)


def _resident(shape, index_map):
    return pl.BlockSpec(shape, index_map, pipeline_mode=pl.Buffered(1))


def _with_weight_casts(body, n_in, n_out, n_cast):
    def wrapped(*refs):
        ins = refs[:n_in]
        cast_src = refs[n_in:n_in + n_cast]
        outs = refs[n_in + n_cast:n_in + n_cast + n_out]
        cast_dst = refs[n_in + n_cast + n_out:n_in + 2 * n_cast + n_out]
        scratch = refs[n_in + 2 * n_cast + n_out:]
        for src, dst in zip(cast_src, cast_dst):
            dst[...] = src[...].astype(BF16)
        body(*ins, *outs, *scratch)
    return wrapped


def _cast_specs(casts, n_outer, n_inner):
    in_specs, out_specs, out_shapes, args = [], [], [], []
    for w, layer in casts:
        _, r, c = w.shape
        rb = r // (n_outer * n_inner)
        in_specs.append(pl.BlockSpec((None, rb, c), lambda b, t, layer=layer: (layer, b * n_inner + t, 0)))
        out_specs.append(pl.BlockSpec((rb, c), lambda b, t: (b * n_inner + t, 0)))
        out_shapes.append(jax.ShapeDtypeStruct((r, c), BF16))
        args.append(w)
    return in_specs, out_specs, out_shapes, args


def _rms(x, g):
    ms = jnp.mean(x * x, axis=-1, keepdims=True)
    return x * lax.rsqrt(ms + EPS) * g


def _rms_mod(x, g, shift, scale):
    ms = jnp.mean(x * x, axis=-1, keepdims=True)
    return x * lax.rsqrt(ms + EPS) * (g * (1.0 + scale)) + shift


def _layernorm(x, g, b):
    mu = jnp.mean(x, axis=-1, keepdims=True)
    xc = x - mu
    var = jnp.mean(xc * xc, axis=-1, keepdims=True)
    return xc * lax.rsqrt(var + EPS) * g + b


def _silu(x):
    return x * jax.nn.sigmoid(x)


def _dot(a, b):
    return jnp.dot(a, b, preferred_element_type=F32)


def _ada_kernel(c_ref, w_ref, b_ref, o_ref):
    c = c_ref[...]
    layer = pl.program_id(0)
    res = _dot(_silu(c).astype(BF16), w_ref[...].astype(BF16))
    for i in range(ADA_GROUP):
        cols = slice(i * D_MODEL, (i + 1) * D_MODEL)
        o_ref[i] = res[:, cols] + b_ref[pl.ds(layer, 1), cols]


def _ada_call(c_all, w_ada, b_ada):
    n = c_all.shape[0]
    return pl.pallas_call(
        _ada_kernel,
        grid=(DEPTH, N_MOD // ADA_GROUP),
        in_specs=[
            pl.BlockSpec((n, D_MODEL), lambda l, j: (0, 0)),
            pl.BlockSpec((None, D_MODEL, ADA_GROUP * D_MODEL), lambda l, j: (l, 0, j)),
            pl.BlockSpec((DEPTH, ADA_GROUP * D_MODEL), lambda l, j: (0, j)),
        ],
        out_specs=pl.BlockSpec((None, ADA_GROUP, n, D_MODEL), lambda l, j: (l, j, 0, 0)),
        out_shape=jax.ShapeDtypeStruct((DEPTH, N_MOD, n, D_MODEL), F32),
        compiler_params=_params(),
        name="adaln",
    )(c_all, w_ada, b_ada)


def _mlp_body(x2d, h_bf, gate, w1_ref, w2_ref):
    acc = None
    for j in range(D_FF // FF_CHUNK):
        cols = slice(j * FF_CHUNK, (j + 1) * FF_CHUNK)
        hid = _dot(h_bf, w1_ref[:, cols])
        hid = jnp.square(jnp.maximum(hid, 0.0)).astype(BF16)
        part = _dot(hid, w2_ref[cols, :])
        acc = part if acc is None else acc + part
    return x2d + gate * acc


def _mlp_prompt_kernel(x_ref, mod_ref, g_ref, w1_ref, w2_ref, fg_ref, xs_ref, mods_ref, o_ref, os_ref,
                       *, final):
    b = pl.program_id(0)
    x = x_ref[...]
    shift = mod_ref[3, pl.ds(b, 1), :]
    scale = mod_ref[4, pl.ds(b, 1), :]
    gate = mod_ref[5, pl.ds(b, 1), :]
    h = _rms_mod(x, g_ref[1:2, :], shift, scale).astype(BF16)
    y = _mlp_body(x, h, gate, w1_ref, w2_ref)
    if final:
        y = _rms(y, fg_ref[...])
    o_ref[...] = y

    @pl.when(jnp.logical_and(b == pl.num_programs(0) - 1,
                             pl.program_id(1) == pl.num_programs(1) - 1))
    def _():
        t, s, d = xs_ref.shape
        half = t // 2
        for p in range(0, t, half):
            xs = xs_ref[p:p + half]
            hs = _rms_mod(xs, g_ref[1:2, :], mods_ref[3][None], mods_ref[4][None])
            hs = hs.reshape(half * s, d).astype(BF16)
            gs = jnp.broadcast_to(mods_ref[5][None], (half, s, d)).reshape(half * s, d)
            ys = _mlp_body(xs.reshape(half * s, d), hs, gs, w1_ref, w2_ref)
            if final:
                ys = _rms(ys, fg_ref[...])
            os_ref[p:p + half] = ys.reshape(half, s, d)


def _mlp_call(x, xs, mod, norm_g, w1, w2, final_g, l, final, casts):
    nb, seq, d = x.shape
    ts, ns, _ = xs.shape
    n_prompt_blk = (mod.shape[2] - nb) // nb
    n_tiles = seq // MLP_ROW_TILE
    c_in, c_out, c_shapes, c_args = _cast_specs(casts, nb, n_tiles)
    outs = pl.pallas_call(
        _with_weight_casts(functools.partial(_mlp_prompt_kernel, final=final), 8, 2, len(casts)),
        grid=(nb, n_tiles),
        in_specs=[
            pl.BlockSpec((None, MLP_ROW_TILE, d), lambda b, t: (b, t, 0)),
            pl.BlockSpec((None, N_MOD, nb, d), lambda b, t: (l, 0, n_prompt_blk, 0)),
            pl.BlockSpec((None, 2, d), lambda b, t: (l, 0, 0)),
            _resident((d, D_FF), lambda b, t: (0, 0)),
            _resident((D_FF, d), lambda b, t: (0, 0)),
            pl.BlockSpec((1, d), lambda b, t: (0, 0)),
            _resident((ts, ns, d), lambda b, t: (0, 0, 0)),
            _resident((None, N_MOD, ns, d), lambda b, t: (l, 0, 0, 0)),
        ] + c_in,
        out_specs=[pl.BlockSpec((None, MLP_ROW_TILE, d), lambda b, t: (b, t, 0)),
                   pl.BlockSpec((ts, ns, d), lambda b, t: (0, 0, 0))] + c_out,
        out_shape=[jax.ShapeDtypeStruct(x.shape, F32), jax.ShapeDtypeStruct(xs.shape, F32)] + c_shapes,
        compiler_params=_params(),
        name=f"mlp_{l}",
    )(x, mod, norm_g, w1, w2, final_g, xs, mod, *c_args)
    return outs[0], outs[1], outs[2:]


def _even_prompt_kernel(x_ref, mod_ref, g_ref, w_in_ref, caw_ref, cab_ref, lng_ref, lnb_ref,
                        cbw_ref, w_out_ref, o_ref, sa_ref, sb_ref, abuf, bbuf, shbuf, cbuf, *, row):
    b = pl.program_id(0)
    t = pl.program_id(1)
    rows = x_ref.shape[0]
    sub = CONV_SUB_ROWS
    ext = sub + A_HIST_ROWS
    first_a = A_HIST_ROWS - (CONV_A - 1)
    first_b = B_HIST_ROWS - (CONV_B - 1)

    @pl.when(t == 0)
    def _():
        abuf[0:A_HIST_ROWS, :] = jnp.zeros((A_HIST_ROWS, D_A), F32)
        bbuf[0:B_HIST_ROWS, :] = jnp.zeros((B_HIST_ROWS, D_B), F32)

    shift = mod_ref[0, pl.ds(b, 1), :]
    scale = mod_ref[1, pl.ds(b, 1), :]
    gate = mod_ref[2, pl.ds(b, 1), :]

    for j in range(rows // sub):
        r0 = j * sub
        x = x_ref[r0:r0 + sub, :]
        h = _rms_mod(x, g_ref[0:1, :], shift, scale).astype(BF16)
        z = _dot(h, w_in_ref[...])

        abuf[A_HIST_ROWS + r0:A_HIST_ROWS + r0 + sub, :] = (
            z[:, 0:D_A] * jax.nn.sigmoid(z[:, D_A:2 * D_A]))
        a_ext = abuf[r0:r0 + ext, :]
        for s in range(1, SUBLANES):
            shbuf[j % 2, s - 1] = pltpu.roll(a_ext, ext - s, axis=0)
        for c in range(D_A // LANES):
            lanes = slice(c * LANES, (c + 1) * LANES)
            for rb in range(sub // CONV_ROW_BLOCK):
                acc = jnp.broadcast_to(cab_ref[row:row + 1, lanes], (CONV_ROW_BLOCK, LANES))
                for k in range(CONV_A):
                    q, s = divmod(first_a + k, SUBLANES)
                    start = q * SUBLANES + rb * CONV_ROW_BLOCK
                    if s == 0:
                        tap = abuf[r0 + start:r0 + start + CONV_ROW_BLOCK, lanes]
                    else:
                        tap = shbuf[j % 2, s - 1, start:start + CONV_ROW_BLOCK, lanes]
                    acc = acc + caw_ref[k:k + 1, lanes] * tap
                cbuf[r0 + rb * CONV_ROW_BLOCK:r0 + (rb + 1) * CONV_ROW_BLOCK, lanes] = acc
        a_out = _silu(_layernorm(cbuf[r0:r0 + sub, :], lng_ref[row:row + 1, :], lnb_ref[row:row + 1, :]))

        bbuf[B_HIST_ROWS + r0:B_HIST_ROWS + r0 + sub, :] = z[:, 4 * D_A:5 * D_A] * z[:, 2 * D_A:3 * D_A]
        accb = cbw_ref[0:1, :] * bbuf[pl.ds(r0 + first_b, sub), :]
        for k in range(1, CONV_B):
            accb = accb + cbw_ref[k:k + 1, :] * bbuf[pl.ds(r0 + first_b + k, sub), :]
        b_out = z[:, 3 * D_A:4 * D_A] * accb

        y = _dot(jnp.concatenate([a_out, b_out], axis=-1).astype(BF16), w_out_ref[...])
        o_ref[r0:r0 + sub, :] = x + gate * y

    @pl.when(t == pl.num_programs(1) - 1)
    def _():
        sa_ref[...] = abuf[pl.ds(rows + first_a, CONV_A - 1), :]
        sb_ref[...] = bbuf[pl.ds(rows + first_b, CONV_B - 1), :]

    abuf[0:A_HIST_ROWS, :] = abuf[rows:rows + A_HIST_ROWS, :]
    bbuf[0:B_HIST_ROWS, :] = bbuf[rows:rows + B_HIST_ROWS, :]


def _even_prompt_call(x, mod, norm_g, w_in, caw, cab, lng, lnb, cbw, w_out, l, casts):
    nb, seq, d = x.shape
    e = l // 2
    n_prompt_blk = (mod.shape[2] - nb) // nb
    n_tiles = seq // ROW_TILE
    c_in, c_out, c_shapes, c_args = _cast_specs(casts, nb, n_tiles)
    vec = lambda width: pl.BlockSpec((cab.shape[0], width), lambda b, t: (0, 0))
    outs = pl.pallas_call(
        _with_weight_casts(functools.partial(_even_prompt_kernel, row=e), 10, 3, len(casts)),
        grid=(nb, n_tiles),
        in_specs=[
            pl.BlockSpec((None, ROW_TILE, d), lambda b, t: (b, t, 0)),
            pl.BlockSpec((None, N_MOD, nb, d), lambda b, t: (l, 0, n_prompt_blk, 0)),
            pl.BlockSpec((None, 2, d), lambda b, t: (l, 0, 0)),
            _resident((d, D_IN_EVEN), lambda b, t: (0, 0)),
            pl.BlockSpec((None, CONV_A, D_A), lambda b, t: (e, 0, 0)),
            vec(D_A), vec(D_A), vec(D_A),
            pl.BlockSpec((None, CONV_B, D_B), lambda b, t: (e, 0, 0)),
            _resident((D_A + D_B, d), lambda b, t: (0, 0)),
        ] + c_in,
        out_specs=[
            pl.BlockSpec((None, ROW_TILE, d), lambda b, t: (b, t, 0)),
            pl.BlockSpec((None, CONV_A - 1, D_A), lambda b, t: (b, 0, 0)),
            pl.BlockSpec((None, CONV_B - 1, D_B), lambda b, t: (b, 0, 0)),
        ] + c_out,
        out_shape=[
            jax.ShapeDtypeStruct(x.shape, F32),
            jax.ShapeDtypeStruct((nb, CONV_A - 1, D_A), F32),
            jax.ShapeDtypeStruct((nb, CONV_B - 1, D_B), F32),
        ] + c_shapes,
        scratch_shapes=[
            pltpu.VMEM((ROW_TILE + A_HIST_ROWS, D_A), F32),
            pltpu.VMEM((ROW_TILE + B_HIST_ROWS, D_B), F32),
            pltpu.VMEM((2, SUBLANES - 1, CONV_SUB_ROWS + A_HIST_ROWS, D_A), F32),
            pltpu.VMEM((ROW_TILE, D_A), F32),
        ],
        compiler_params=_params(),
        name=f"even_prompt_{l}",
    )(x, mod, norm_g, w_in, caw, cab, lng, lnb, cbw, w_out, *c_args)
    return outs[0], outs[1], outs[2], outs[3:]


def _even_sample_kernel(x_ref, mod_ref, g_ref, ha_ref, hb_ref, w_in_ref, caw_ref, cab_ref,
                        lng_ref, lnb_ref, cbw_ref, w_out_ref, o_ref, sa_ref, sb_ref, *, row):
    x = x_ref[...]
    t, s, d = x.shape
    h = _rms_mod(x, g_ref[0:1, :], mod_ref[0][None], mod_ref[1][None])
    z = _dot(h.reshape(t * s, d).astype(BF16), w_in_ref[...]).reshape(t, s, D_IN_EVEN)

    a = z[:, :, 0:D_A] * jax.nn.sigmoid(z[:, :, D_A:2 * D_A])
    a_ext = jnp.concatenate([ha_ref[...], a], axis=0)
    acc = jnp.broadcast_to(cab_ref[row:row + 1, :][None], (t, s, D_A))
    for k in range(CONV_A):
        acc = acc + caw_ref[k:k + 1, :][None] * a_ext[k:k + t]
    a_out = _silu(_layernorm(acc, lng_ref[row:row + 1, :][None], lnb_ref[row:row + 1, :][None]))
    sa_ref[...] = a_ext[t:]

    bx = z[:, :, 4 * D_A:5 * D_A] * z[:, :, 2 * D_A:3 * D_A]
    b_ext = jnp.concatenate([hb_ref[...], bx], axis=0)
    accb = cbw_ref[0:1, :][None] * b_ext[0:t]
    for k in range(1, CONV_B):
        accb = accb + cbw_ref[k:k + 1, :][None] * b_ext[k:k + t]
    b_out = z[:, :, 3 * D_A:4 * D_A] * accb
    sb_ref[...] = b_ext[t:]

    cat = jnp.concatenate([a_out, b_out], axis=-1).reshape(t * s, D_A + D_B).astype(BF16)
    y = _dot(cat, w_out_ref[...]).reshape(t, s, d)
    o_ref[...] = x + mod_ref[2][None] * y


def _even_sample_call(x, mod, norm_g, hist_a, hist_b, w_in, caw, cab, lng, lnb, cbw, w_out, l):
    t, ns, d = x.shape
    e = l // 2
    vec = lambda width: pl.BlockSpec((cab.shape[0], width), lambda s, _: (0, 0))
    return pl.pallas_call(
        functools.partial(_even_sample_kernel, row=e),
        grid=(ns // SEQ_BLOCK, 1),
        in_specs=[
            pl.BlockSpec((t, SEQ_BLOCK, d), lambda s, _: (0, s, 0)),
            pl.BlockSpec((None, N_MOD, SEQ_BLOCK, d), lambda s, _: (l, 0, s, 0)),
            pl.BlockSpec((None, 2, d), lambda s, _: (l, 0, 0)),
            pl.BlockSpec((None, CONV_A - 1, SEQ_BLOCK, D_A), lambda s, _: (e, 0, s, 0)),
            pl.BlockSpec((None, CONV_B - 1, SEQ_BLOCK, D_B), lambda s, _: (e, 0, s, 0)),
            _resident((d, D_IN_EVEN), lambda s, _: (0, 0)),
            pl.BlockSpec((None, CONV_A, D_A), lambda s, _: (e, 0, 0)),
            vec(D_A), vec(D_A), vec(D_A),
            pl.BlockSpec((None, CONV_B, D_B), lambda s, _: (e, 0, 0)),
            _resident((D_A + D_B, d), lambda s, _: (0, 0)),
        ],
        out_specs=[
            pl.BlockSpec((t, SEQ_BLOCK, d), lambda s, _: (0, s, 0)),
            pl.BlockSpec((CONV_A - 1, SEQ_BLOCK, D_A), lambda s, _: (0, s, 0)),
            pl.BlockSpec((CONV_B - 1, SEQ_BLOCK, D_B), lambda s, _: (0, s, 0)),
        ],
        out_shape=[
            jax.ShapeDtypeStruct(x.shape, F32),
            jax.ShapeDtypeStruct((CONV_A - 1, ns, D_A), F32),
            jax.ShapeDtypeStruct((CONV_B - 1, ns, D_B), F32),
        ],
        compiler_params=_params(),
        name=f"even_sample_{l}",
    )(x, mod, norm_g, hist_a, hist_b, w_in, caw, cab, lng, lnb, cbw, w_out)


def _odd_prompt_kernel(x_ref, mod_ref, g_ref, w_in_ref, b_in_ref, lng_ref, lnb_ref, ws_ref,
                       sbias_ref, w_out_ref, xs_ref, mods_ref, wm_ref, o_ref, cv_ref, os_ref, cvs_ref,
                       *, row):
    b = pl.program_id(0)
    t = pl.program_id(1)

    @pl.when(b == pl.num_programs(0) - 1)
    def _():
        _odd_sample_body(xs_ref, mods_ref, g_ref, w_in_ref, b_in_ref, lng_ref, lnb_ref, wm_ref,
                         sbias_ref, w_out_ref, os_ref, cvs_ref, row)

    rows = x_ref.shape[0]
    sub = ODD_SUB_ROWS
    shift = mod_ref[0, pl.ds(b, 1), :]
    scale = mod_ref[1, pl.ds(b, 1), :]
    gate = mod_ref[2, pl.ds(b, 1), :]
    causal = (lax.broadcasted_iota(jnp.int32, (CHUNK, CHUNK), 0)
              >= lax.broadcasted_iota(jnp.int32, (CHUNK, CHUNK), 1))
    ws = [jnp.where(causal, ws_ref[hd], 0.0).astype(BF16) for hd in range(C_HEADS)]

    for j in range(rows // sub):
        r0 = j * sub
        x = x_ref[r0:r0 + sub, :]
        h = _rms_mod(x, g_ref[0:1, :], shift, scale).astype(BF16)
        z = jax.nn.gelu(_dot(h, w_in_ref[...]) + b_in_ref[row:row + 1, :])
        u = z[:, 0:D_C]
        v = _layernorm(z[:, D_C:2 * D_C], lng_ref[row:row + 1, :], lnb_ref[row:row + 1, :])

        if r0 + sub == rows:
            @pl.when(t == pl.num_programs(1) - 1)
            def _():
                cv_ref[...] = v[sub - CHUNK:sub, :]

        v_bf = v.astype(BF16)
        s_rows = []
        for c in range(sub // CHUNK):
            heads = [
                _dot(ws[hd], v_bf[c * CHUNK:(c + 1) * CHUNK, hd * C_HEAD_DIM:(hd + 1) * C_HEAD_DIM])
                for hd in range(C_HEADS)
            ]
            s_rows.append(jnp.concatenate(heads, axis=-1) + sbias_ref[...])
        s = jnp.concatenate(s_rows, axis=0)
        y = _dot((u * s).astype(BF16), w_out_ref[...])
        o_ref[r0:r0 + sub, :] = x + gate * y


def _odd_call(x, xs, mod, norm_g, w_in, b_in, lng, lnb, w_s, sbias, wm, w_out, l, casts):
    nb, seq, d = x.shape
    ts, ns, _ = xs.shape
    o = l // 2
    n_prompt_blk = (mod.shape[2] - nb) // nb
    n_tiles = seq // ROW_TILE
    seq_blk = ns // n_tiles
    sblk = lambda b, t: jnp.where(b == nb - 1, t, 0)
    c_in, c_out, c_shapes, c_args = _cast_specs(casts, nb, n_tiles)
    outs = pl.pallas_call(
        _with_weight_casts(functools.partial(_odd_prompt_kernel, row=o), 13, 4, len(casts)),
        grid=(nb, n_tiles),
        in_specs=[
            pl.BlockSpec((None, ROW_TILE, d), lambda b, t: (b, t, 0)),
            pl.BlockSpec((None, N_MOD, nb, d), lambda b, t: (l, 0, n_prompt_blk, 0)),
            pl.BlockSpec((None, 2, d), lambda b, t: (l, 0, 0)),
            _resident((d, 2 * D_C), lambda b, t: (0, 0)),
            pl.BlockSpec(b_in.shape, lambda b, t: (0, 0)),
            pl.BlockSpec(lng.shape, lambda b, t: (0, 0)),
            pl.BlockSpec(lnb.shape, lambda b, t: (0, 0)),
            pl.BlockSpec((None, C_HEADS, CHUNK, CHUNK), lambda b, t: (o, 0, 0, 0)),
            pl.BlockSpec((None, CHUNK, D_C), lambda b, t: (o, 0, 0)),
            _resident((D_C, d), lambda b, t: (0, 0)),
            pl.BlockSpec((ts, seq_blk, d), lambda b, t: (0, sblk(b, t), 0)),
            pl.BlockSpec((None, N_MOD // 2, seq_blk, d), lambda b, t: (l, 0, sblk(b, t), 0)),
            pl.BlockSpec((None, ts, ts, D_C), lambda b, t: (o, 0, 0, 0)),
        ] + c_in,
        out_specs=[
            pl.BlockSpec((None, ROW_TILE, d), lambda b, t: (b, t, 0)),
            pl.BlockSpec((None, CHUNK, D_C), lambda b, t: (b, 0, 0)),
            pl.BlockSpec((ts, seq_blk, d), lambda b, t: (0, sblk(b, t), 0)),
            pl.BlockSpec((ts, seq_blk, D_C), lambda b, t: (0, sblk(b, t), 0)),
        ] + c_out,
        out_shape=[
            jax.ShapeDtypeStruct(x.shape, F32),
            jax.ShapeDtypeStruct((nb, CHUNK, D_C), F32),
            jax.ShapeDtypeStruct(xs.shape, F32),
            jax.ShapeDtypeStruct((ts, ns, D_C), F32),
        ] + c_shapes,
        compiler_params=_params(),
        name=f"odd_{l}",
    )(x, mod, norm_g, w_in, b_in, lng, lnb, w_s, sbias, w_out, xs, mod, wm, *c_args)
    return outs[0], outs[1], outs[2], outs[3], outs[4:]


def _odd_sample_body(x_ref, mod_ref, g_ref, w_in_ref, b_in_ref, lng_ref, lnb_ref, wm_ref,
                     sbias_ref, w_out_ref, o_ref, cv_ref, row):
    x = x_ref[...]
    t, s, d = x.shape
    h = _rms_mod(x, g_ref[0:1, :], mod_ref[0][None], mod_ref[1][None])
    z = jax.nn.gelu(_dot(h.reshape(t * s, d).astype(BF16), w_in_ref[...]) + b_in_ref[row:row + 1, :])
    z = z.reshape(t, s, 2 * D_C)
    u = z[:, :, 0:D_C]
    v = _layernorm(z[:, :, D_C:2 * D_C], lng_ref[row:row + 1, :][None], lnb_ref[row:row + 1, :][None])
    cv_ref[...] = v
    gated = []
    for i in range(t):
        s_i = jnp.broadcast_to(sbias_ref[i:i + 1, :], (s, D_C))
        for j in range(i + 1):
            s_i = s_i + wm_ref[i, j:j + 1, :] * v[j]
        gated.append(u[i] * s_i)
    us = jnp.stack(gated, axis=0).reshape(t * s, D_C).astype(BF16)
    y = _dot(us, w_out_ref[...]).reshape(t, s, d)
    o_ref[...] = x + mod_ref[2][None] * y


def kernel(x_prompt, x_sample, state_conv_a, state_conv_b, c_prompt, c_sample, w_in_ab, conv_a_w, conv_a_b, ln_a_g, ln_a_b, conv_b_w, w_out_ab, w_in_c, b_in_c, ln_v_g, ln_v_b, w_s, b_s, w_out_c, w_ada, b_ada, norm_g, w_ff1, w_ff2, final_g):
    dec_seq = x_sample.shape[1]

    mix_in_bf = w_in_ab[0].astype(BF16)
    mix_out_bf = w_out_ab[0].astype(BF16)

    cab3, lnag3, lnab3 = conv_a_b, ln_a_g, ln_a_b
    binc3, lnvg3, lnvb3 = b_in_c, ln_v_g, ln_v_b
    fg2 = final_g.reshape(1, D_MODEL)

    sbias = jnp.repeat(jnp.swapaxes(b_s, 1, 2), C_HEAD_DIM, axis=2)
    wm = jnp.repeat(jnp.transpose(w_s[:, :, :dec_seq, :dec_seq], (0, 2, 3, 1)),
                    C_HEAD_DIM, axis=3)

    mod = _ada_call(jnp.concatenate([c_sample, c_prompt], axis=0), w_ada, b_ada)

    xp = x_prompt
    xs = jnp.transpose(x_sample, (1, 0, 2))
    ha = jnp.transpose(state_conv_a, (0, 2, 1, 3))
    hb = jnp.transpose(state_conv_b, (0, 2, 1, 3))

    a_p, b_p, v_p, a_s, b_s_out, v_s = [], [], [], [], [], []
    for l in range(DEPTH):
        final = l == DEPTH - 1
        ff_casts = [(w_ff1, l), (w_ff2, l)]
        if l % 2 == 0:
            xp, sa, sb, (ff1_bf, ff2_bf) = _even_prompt_call(
                xp, mod, norm_g, mix_in_bf, conv_a_w, cab3, lnag3, lnab3, conv_b_w, mix_out_bf, l, ff_casts)
            a_p.append(sa)
            b_p.append(sb)
            xs, sa, sb = _even_sample_call(xs, mod, norm_g, ha, hb, mix_in_bf, conv_a_w, cab3, lnag3,
                                           lnab3, conv_b_w, mix_out_bf, l)
            a_s.append(jnp.transpose(sa, (1, 0, 2)))
            b_s_out.append(jnp.transpose(sb, (1, 0, 2)))
        else:
            xp, cv, xs, cvs, (ff1_bf, ff2_bf) = _odd_call(
                xp, xs, mod, norm_g, mix_in_bf, binc3, lnvg3, lnvb3, w_s, sbias, wm, mix_out_bf, l, ff_casts)
            v_p.append(cv)
            v_s.append(jnp.transpose(cvs, (1, 0, 2)))
        if final:
            mix_casts = []
        elif l % 2 == 0:
            mix_casts = [(w_in_c, l // 2), (w_out_c, l // 2)]
        else:
            mix_casts = [(w_in_ab, (l + 1) // 2), (w_out_ab, (l + 1) // 2)]
        xp, xs, next_mix = _mlp_call(xp, xs, mod, norm_g, ff1_bf, ff2_bf, fg2, l, final, mix_casts)
        if next_mix:
            mix_in_bf, mix_out_bf = next_mix

    return (xp, jnp.transpose(xs, (1, 0, 2)), jnp.stack(a_p), jnp.stack(a_s), jnp.stack(b_p),
            jnp.stack(b_s_out), jnp.stack(v_p), jnp.stack(v_s))
```

```python
import functools

import jax
import jax.numpy as jnp
from jax import lax
from jax.experimental import pallas as pl
from jax.experimental.pallas import tpu as pltpu

D_MODEL = 1024
DEPTH = 4
D_A = 512
D_B = 512
CONV_A = 31
CONV_B = 3
D_IN_EVEN = 2 * D_A + 3 * D_B
D_C = 1024
C_HEADS = 8
C_HEAD_DIM = 128
CHUNK = 128
D_FF = 4096
N_MOD = 6
EPS = 1e-6

BF16 = jnp.bfloat16
F32 = jnp.float32

VMEM_LIMIT_BYTES = 58 * 1024 * 1024
SUBLANES = 8
ROW_TILE = 1024
ODD_SUB_ROWS = 512
MLP_ROW_TILE = 1024
SEQ_BLOCK = 64
FF_CHUNK = 1024
ADA_GROUP = 3
A_HIST_ROWS = 32
B_HIST_ROWS = 8
LANES = 128
CONV_SUB_ROWS = 256
CONV_ROW_BLOCK = 128


def _params():
    return pltpu.CompilerParams(
        dimension_semantics=("arbitrary", "arbitrary"),
        vmem_limit_bytes=VMEM_LIMIT_BYTES)


def _resident(shape, index_map):
    return pl.BlockSpec(shape, index_map, pipeline_mode=pl.Buffered(1))


def _with_weight_casts(body, n_in, n_out, n_cast):
    def wrapped(*refs):
        ins = refs[:n_in]
        cast_src = refs[n_in:n_in + n_cast]
        outs = refs[n_in + n_cast:n_in + n_cast + n_out]
        cast_dst = refs[n_in + n_cast + n_out:n_in + 2 * n_cast + n_out]
        scratch = refs[n_in + 2 * n_cast + n_out:]
        for src, dst in zip(cast_src, cast_dst):
            dst[...] = src[...].astype(BF16)
        body(*ins, *outs, *scratch)
    return wrapped


def _cast_specs(casts, n_outer, n_inner):
    in_specs, out_specs, out_shapes, args = [], [], [], []
    for w, layer in casts:
        _, r, c = w.shape
        rb = r // (n_outer * n_inner)
        in_specs.append(pl.BlockSpec((None, rb, c), lambda b, t, layer=layer: (layer, b * n_inner + t, 0)))
        out_specs.append(pl.BlockSpec((rb, c), lambda b, t: (b * n_inner + t, 0)))
        out_shapes.append(jax.ShapeDtypeStruct((r, c), BF16))
        args.append(w)
    return in_specs, out_specs, out_shapes, args


def _rms(x, g):
    ms = jnp.mean(x * x, axis=-1, keepdims=True)
    return x * lax.rsqrt(ms + EPS) * g


def _rms_mod(x, g, shift, scale):
    ms = jnp.mean(x * x, axis=-1, keepdims=True)
    return x * lax.rsqrt(ms + EPS) * (g * (1.0 + scale)) + shift


def _layernorm(x, g, b):
    mu = jnp.mean(x, axis=-1, keepdims=True)
    xc = x - mu
    var = jnp.mean(xc * xc, axis=-1, keepdims=True)
    return xc * lax.rsqrt(var + EPS) * g + b


def _silu(x):
    return x * jax.nn.sigmoid(x)


def _dot(a, b):
    return jnp.dot(a, b, preferred_element_type=F32)


def _ada_kernel(c_ref, w_ref, b_ref, o_ref):
    c = c_ref[...]
    layer = pl.program_id(0)
    res = _dot(_silu(c).astype(BF16), w_ref[...].astype(BF16))
    for i in range(ADA_GROUP):
        cols = slice(i * D_MODEL, (i + 1) * D_MODEL)
        o_ref[i] = res[:, cols] + b_ref[pl.ds(layer, 1), cols]


def _ada_call(c_all, w_ada, b_ada):
    n = c_all.shape[0]
    return pl.pallas_call(
        _ada_kernel,
        grid=(DEPTH, N_MOD // ADA_GROUP),
        in_specs=[
            pl.BlockSpec((n, D_MODEL), lambda l, j: (0, 0)),
            pl.BlockSpec((None, D_MODEL, ADA_GROUP * D_MODEL), lambda l, j: (l, 0, j)),
            pl.BlockSpec((DEPTH, ADA_GROUP * D_MODEL), lambda l, j: (0, j)),
        ],
        out_specs=pl.BlockSpec((None, ADA_GROUP, n, D_MODEL), lambda l, j: (l, j, 0, 0)),
        out_shape=jax.ShapeDtypeStruct((DEPTH, N_MOD, n, D_MODEL), F32),
        compiler_params=_params(),
        name="adaln",
    )(c_all, w_ada, b_ada)


def _mlp_body(x2d, h_bf, gate, w1_ref, w2_ref):
    acc = None
    for j in range(D_FF // FF_CHUNK):
        cols = slice(j * FF_CHUNK, (j + 1) * FF_CHUNK)
        hid = _dot(h_bf, w1_ref[:, cols])
        hid = jnp.square(jnp.maximum(hid, 0.0)).astype(BF16)
        part = _dot(hid, w2_ref[cols, :])
        acc = part if acc is None else acc + part
    return x2d + gate * acc


def _mlp_prompt_kernel(x_ref, mod_ref, g_ref, w1_ref, w2_ref, fg_ref, xs_ref, mods_ref, o_ref, os_ref,
                       *, final):
    b = pl.program_id(0)
    x = x_ref[...]
    shift = mod_ref[3, pl.ds(b, 1), :]
    scale = mod_ref[4, pl.ds(b, 1), :]
    gate = mod_ref[5, pl.ds(b, 1), :]
    h = _rms_mod(x, g_ref[1:2, :], shift, scale).astype(BF16)
    y = _mlp_body(x, h, gate, w1_ref, w2_ref)
    if final:
        y = _rms(y, fg_ref[...])
    o_ref[...] = y

    @pl.when(jnp.logical_and(b == pl.num_programs(0) - 1,
                             pl.program_id(1) == pl.num_programs(1) - 1))
    def _():
        t, s, d = xs_ref.shape
        half = t // 2
        for p in range(0, t, half):
            xs = xs_ref[p:p + half]
            hs = _rms_mod(xs, g_ref[1:2, :], mods_ref[0][None], mods_ref[1][None])
            hs = hs.reshape(half * s, d).astype(BF16)
            gs = jnp.broadcast_to(mods_ref[2][None], (half, s, d)).reshape(half * s, d)
            ys = _mlp_body(xs.reshape(half * s, d), hs, gs, w1_ref, w2_ref)
            if final:
                ys = _rms(ys, fg_ref[...])
            os_ref[p:p + half] = ys.reshape(half, s, d)


def _mlp_call(x, xs, mod, norm_g, w1, w2, final_g, l, final, casts):
    nb, seq, d = x.shape
    ts, ns, _ = xs.shape
    n_prompt_blk = (mod.shape[2] - nb) // nb
    n_tiles = seq // MLP_ROW_TILE
    c_in, c_out, c_shapes, c_args = _cast_specs(casts, nb, n_tiles)
    outs = pl.pallas_call(
        _with_weight_casts(functools.partial(_mlp_prompt_kernel, final=final), 8, 2, len(casts)),
        grid=(nb, n_tiles),
        in_specs=[
            pl.BlockSpec((None, MLP_ROW_TILE, d), lambda b, t: (b, t, 0)),
            pl.BlockSpec((None, N_MOD, nb, d), lambda b, t: (l, 0, n_prompt_blk, 0)),
            pl.BlockSpec((None, 2, d), lambda b, t: (l, 0, 0)),
            _resident((d, D_FF), lambda b, t: (0, 0)),
            _resident((D_FF, d), lambda b, t: (0, 0)),
            pl.BlockSpec((1, d), lambda b, t: (0, 0)),
            _resident((ts, ns, d), lambda b, t: (0, 0, 0)),
            _resident((None, N_MOD // 2, ns, d), lambda b, t: (l, 1, 0, 0)),
        ] + c_in,
        out_specs=[pl.BlockSpec((None, MLP_ROW_TILE, d), lambda b, t: (b, t, 0)),
                   pl.BlockSpec((ts, ns, d), lambda b, t: (0, 0, 0))] + c_out,
        out_shape=[jax.ShapeDtypeStruct(x.shape, F32), jax.ShapeDtypeStruct(xs.shape, F32)] + c_shapes,
        compiler_params=_params(),
        name=f"mlp_{l}",
    )(x, mod, norm_g, w1, w2, final_g, xs, mod, *c_args)
    return outs[0], outs[1], outs[2:]


def _even_prompt_kernel(x_ref, mod_ref, g_ref, w_in_ref, caw_ref, cab_ref, lng_ref, lnb_ref,
                        cbw_ref, w_out_ref, o_ref, sa_ref, sb_ref, abuf, bbuf, shbuf, cbuf, *, row):
    b = pl.program_id(0)
    t = pl.program_id(1)
    rows = x_ref.shape[0]
    sub = CONV_SUB_ROWS
    ext = sub + A_HIST_ROWS
    first_a = A_HIST_ROWS - (CONV_A - 1)
    first_b = B_HIST_ROWS - (CONV_B - 1)

    @pl.when(t == 0)
    def _():
        abuf[0:A_HIST_ROWS, :] = jnp.zeros((A_HIST_ROWS, D_A), F32)
        bbuf[0:B_HIST_ROWS, :] = jnp.zeros((B_HIST_ROWS, D_B), F32)

    shift = mod_ref[0, pl.ds(b, 1), :]
    scale = mod_ref[1, pl.ds(b, 1), :]
    gate = mod_ref[2, pl.ds(b, 1), :]

    for j in range(rows // sub):
        r0 = j * sub
        x = x_ref[r0:r0 + sub, :]
        h = _rms_mod(x, g_ref[0:1, :], shift, scale).astype(BF16)
        z = _dot(h, w_in_ref[...])

        abuf[A_HIST_ROWS + r0:A_HIST_ROWS + r0 + sub, :] = (
            z[:, 0:D_A] * jax.nn.sigmoid(z[:, D_A:2 * D_A]))
        a_ext = abuf[r0:r0 + ext, :]
        for s in range(1, SUBLANES):
            shbuf[j % 2, s - 1] = pltpu.roll(a_ext, ext - s, axis=0)
        for c in range(D_A // LANES):
            lanes = slice(c * LANES, (c + 1) * LANES)
            for rb in range(sub // CONV_ROW_BLOCK):
                acc = jnp.broadcast_to(cab_ref[row:row + 1, lanes], (CONV_ROW_BLOCK, LANES))
                for k in range(CONV_A):
                    q, s = divmod(first_a + k, SUBLANES)
                    start = q * SUBLANES + rb * CONV_ROW_BLOCK
                    if s == 0:
                        tap = abuf[r0 + start:r0 + start + CONV_ROW_BLOCK, lanes]
                    else:
                        tap = shbuf[j % 2, s - 1, start:start + CONV_ROW_BLOCK, lanes]
                    acc = acc + caw_ref[k:k + 1, lanes] * tap
                cbuf[r0 + rb * CONV_ROW_BLOCK:r0 + (rb + 1) * CONV_ROW_BLOCK, lanes] = acc
        a_out = _silu(_layernorm(cbuf[r0:r0 + sub, :], lng_ref[row:row + 1, :], lnb_ref[row:row + 1, :]))

        bbuf[B_HIST_ROWS + r0:B_HIST_ROWS + r0 + sub, :] = z[:, 4 * D_A:5 * D_A] * z[:, 2 * D_A:3 * D_A]
        accb = cbw_ref[0:1, :] * bbuf[pl.ds(r0 + first_b, sub), :]
        for k in range(1, CONV_B):
            accb = accb + cbw_ref[k:k + 1, :] * bbuf[pl.ds(r0 + first_b + k, sub), :]
        b_out = z[:, 3 * D_A:4 * D_A] * accb

        y = _dot(jnp.concatenate([a_out, b_out], axis=-1).astype(BF16), w_out_ref[...])
        o_ref[r0:r0 + sub, :] = x + gate * y

    @pl.when(t == pl.num_programs(1) - 1)
    def _():
        sa_ref[...] = abuf[pl.ds(rows + first_a, CONV_A - 1), :]
        sb_ref[...] = bbuf[pl.ds(rows + first_b, CONV_B - 1), :]

    abuf[0:A_HIST_ROWS, :] = abuf[rows:rows + A_HIST_ROWS, :]
    bbuf[0:B_HIST_ROWS, :] = bbuf[rows:rows + B_HIST_ROWS, :]


def _even_prompt_call(x, mod, norm_g, w_in, caw, cab, lng, lnb, cbw, w_out, l, casts):
    nb, seq, d = x.shape
    e = l // 2
    n_prompt_blk = (mod.shape[2] - nb) // nb
    n_tiles = seq // ROW_TILE
    c_in, c_out, c_shapes, c_args = _cast_specs(casts, nb, n_tiles)
    vec = lambda width: pl.BlockSpec((cab.shape[0], width), lambda b, t: (0, 0))
    outs = pl.pallas_call(
        _with_weight_casts(functools.partial(_even_prompt_kernel, row=e), 10, 3, len(casts)),
        grid=(nb, n_tiles),
        in_specs=[
            pl.BlockSpec((None, ROW_TILE, d), lambda b, t: (b, t, 0)),
            pl.BlockSpec((None, N_MOD, nb, d), lambda b, t: (l, 0, n_prompt_blk, 0)),
            pl.BlockSpec((None, 2, d), lambda b, t: (l, 0, 0)),
            _resident((d, D_IN_EVEN), lambda b, t: (0, 0)),
            pl.BlockSpec((None, CONV_A, D_A), lambda b, t: (e, 0, 0)),
            vec(D_A), vec(D_A), vec(D_A),
            pl.BlockSpec((None, CONV_B, D_B), lambda b, t: (e, 0, 0)),
            _resident((D_A + D_B, d), lambda b, t: (0, 0)),
        ] + c_in,
        out_specs=[
            pl.BlockSpec((None, ROW_TILE, d), lambda b, t: (b, t, 0)),
            pl.BlockSpec((None, CONV_A - 1, D_A), lambda b, t: (b, 0, 0)),
            pl.BlockSpec((None, CONV_B - 1, D_B), lambda b, t: (b, 0, 0)),
        ] + c_out,
        out_shape=[
            jax.ShapeDtypeStruct(x.shape, F32),
            jax.ShapeDtypeStruct((nb, CONV_A - 1, D_A), F32),
            jax.ShapeDtypeStruct((nb, CONV_B - 1, D_B), F32),
        ] + c_shapes,
        scratch_shapes=[
            pltpu.VMEM((ROW_TILE + A_HIST_ROWS, D_A), F32),
            pltpu.VMEM((ROW_TILE + B_HIST_ROWS, D_B), F32),
            pltpu.VMEM((2, SUBLANES - 1, CONV_SUB_ROWS + A_HIST_ROWS, D_A), F32),
            pltpu.VMEM((ROW_TILE, D_A), F32),
        ],
        compiler_params=_params(),
        name=f"even_prompt_{l}",
    )(x, mod, norm_g, w_in, caw, cab, lng, lnb, cbw, w_out, *c_args)
    return outs[0], outs[1], outs[2], outs[3:]


def _even_sample_kernel(x_ref, mod_ref, g_ref, ha_ref, hb_ref, w_in_ref, caw_ref, cab_ref,
                        lng_ref, lnb_ref, cbw_ref, w_out_ref, o_ref, sa_ref, sb_ref, *, row):
    x = x_ref[...]
    t, s, d = x.shape
    h = _rms_mod(x, g_ref[0:1, :], mod_ref[0][None], mod_ref[1][None])
    z = _dot(h.reshape(t * s, d).astype(BF16), w_in_ref[...]).reshape(t, s, D_IN_EVEN)

    a = z[:, :, 0:D_A] * jax.nn.sigmoid(z[:, :, D_A:2 * D_A])
    a_ext = jnp.concatenate([ha_ref[...], a], axis=0)
    acc = jnp.broadcast_to(cab_ref[row:row + 1, :][None], (t, s, D_A))
    for k in range(CONV_A):
        acc = acc + caw_ref[k:k + 1, :][None] * a_ext[k:k + t]
    a_out = _silu(_layernorm(acc, lng_ref[row:row + 1, :][None], lnb_ref[row:row + 1, :][None]))
    sa_ref[...] = a_ext[t:]

    bx = z[:, :, 4 * D_A:5 * D_A] * z[:, :, 2 * D_A:3 * D_A]
    b_ext = jnp.concatenate([hb_ref[...], bx], axis=0)
    accb = cbw_ref[0:1, :][None] * b_ext[0:t]
    for k in range(1, CONV_B):
        accb = accb + cbw_ref[k:k + 1, :][None] * b_ext[k:k + t]
    b_out = z[:, :, 3 * D_A:4 * D_A] * accb
    sb_ref[...] = b_ext[t:]

    cat = jnp.concatenate([a_out, b_out], axis=-1).reshape(t * s, D_A + D_B).astype(BF16)
    y = _dot(cat, w_out_ref[...]).reshape(t, s, d)
    o_ref[...] = x + mod_ref[2][None] * y


def _even_sample_call(x, mod, norm_g, hist_a, hist_b, w_in, caw, cab, lng, lnb, cbw, w_out, l):
    t, ns, d = x.shape
    e = l // 2
    vec = lambda width: pl.BlockSpec((cab.shape[0], width), lambda s, _: (0, 0))
    return pl.pallas_call(
        functools.partial(_even_sample_kernel, row=e),
        grid=(ns // SEQ_BLOCK, 1),
        in_specs=[
            pl.BlockSpec((t, SEQ_BLOCK, d), lambda s, _: (0, s, 0)),
            pl.BlockSpec((None, N_MOD, SEQ_BLOCK, d), lambda s, _: (l, 0, s, 0)),
            pl.BlockSpec((None, 2, d), lambda s, _: (l, 0, 0)),
            pl.BlockSpec((None, CONV_A - 1, SEQ_BLOCK, D_A), lambda s, _: (e, 0, s, 0)),
            pl.BlockSpec((None, CONV_B - 1, SEQ_BLOCK, D_B), lambda s, _: (e, 0, s, 0)),
            _resident((d, D_IN_EVEN), lambda s, _: (0, 0)),
            pl.BlockSpec((None, CONV_A, D_A), lambda s, _: (e, 0, 0)),
            vec(D_A), vec(D_A), vec(D_A),
            pl.BlockSpec((None, CONV_B, D_B), lambda s, _: (e, 0, 0)),
            _resident((D_A + D_B, d), lambda s, _: (0, 0)),
        ],
        out_specs=[
            pl.BlockSpec((t, SEQ_BLOCK, d), lambda s, _: (0, s, 0)),
            pl.BlockSpec((CONV_A - 1, SEQ_BLOCK, D_A), lambda s, _: (0, s, 0)),
            pl.BlockSpec((CONV_B - 1, SEQ_BLOCK, D_B), lambda s, _: (0, s, 0)),
        ],
        out_shape=[
            jax.ShapeDtypeStruct(x.shape, F32),
            jax.ShapeDtypeStruct((CONV_A - 1, ns, D_A), F32),
            jax.ShapeDtypeStruct((CONV_B - 1, ns, D_B), F32),
        ],
        compiler_params=_params(),
        name=f"even_sample_{l}",
    )(x, mod, norm_g, hist_a, hist_b, w_in, caw, cab, lng, lnb, cbw, w_out)


def _odd_prompt_kernel(x_ref, mod_ref, g_ref, w_in_ref, b_in_ref, lng_ref, lnb_ref, ws_ref,
                       sbias_ref, w_out_ref, xs_ref, mods_ref, wm_ref, o_ref, cv_ref, os_ref, cvs_ref,
                       *, row):
    b = pl.program_id(0)
    t = pl.program_id(1)

    @pl.when(b == pl.num_programs(0) - 1)
    def _():
        _odd_sample_body(xs_ref, mods_ref, g_ref, w_in_ref, b_in_ref, lng_ref, lnb_ref, wm_ref,
                         sbias_ref, w_out_ref, os_ref, cvs_ref, row)

    rows = x_ref.shape[0]
    sub = ODD_SUB_ROWS
    shift = mod_ref[0, pl.ds(b, 1), :]
    scale = mod_ref[1, pl.ds(b, 1), :]
    gate = mod_ref[2, pl.ds(b, 1), :]
    causal = (lax.broadcasted_iota(jnp.int32, (CHUNK, CHUNK), 0)
              >= lax.broadcasted_iota(jnp.int32, (CHUNK, CHUNK), 1))
    ws = [jnp.where(causal, ws_ref[hd], 0.0).astype(BF16) for hd in range(C_HEADS)]

    for j in range(rows // sub):
        r0 = j * sub
        x = x_ref[r0:r0 + sub, :]
        h = _rms_mod(x, g_ref[0:1, :], shift, scale).astype(BF16)
        z = jax.nn.gelu(_dot(h, w_in_ref[...]) + b_in_ref[row:row + 1, :])
        u = z[:, 0:D_C]
        v = _layernorm(z[:, D_C:2 * D_C], lng_ref[row:row + 1, :], lnb_ref[row:row + 1, :])

        if r0 + sub == rows:
            @pl.when(t == pl.num_programs(1) - 1)
            def _():
                cv_ref[...] = v[sub - CHUNK:sub, :]

        v_bf = v.astype(BF16)
        s_rows = []
        for c in range(sub // CHUNK):
            heads = [
                _dot(ws[hd], v_bf[c * CHUNK:(c + 1) * CHUNK, hd * C_HEAD_DIM:(hd + 1) * C_HEAD_DIM])
                for hd in range(C_HEADS)
            ]
            s_rows.append(jnp.concatenate(heads, axis=-1) + sbias_ref[...])
        s = jnp.concatenate(s_rows, axis=0)
        y = _dot((u * s).astype(BF16), w_out_ref[...])
        o_ref[r0:r0 + sub, :] = x + gate * y


def _odd_call(x, xs, mod, norm_g, w_in, b_in, lng, lnb, w_s, sbias, wm, w_out, l, casts):
    nb, seq, d = x.shape
    ts, ns, _ = xs.shape
    o = l // 2
    n_prompt_blk = (mod.shape[2] - nb) // nb
    n_tiles = seq // ROW_TILE
    seq_blk = ns // n_tiles
    sblk = lambda b, t: jnp.where(b == nb - 1, t, 0)
    c_in, c_out, c_shapes, c_args = _cast_specs(casts, nb, n_tiles)
    outs = pl.pallas_call(
        _with_weight_casts(functools.partial(_odd_prompt_kernel, row=o), 13, 4, len(casts)),
        grid=(nb, n_tiles),
        in_specs=[
            pl.BlockSpec((None, ROW_TILE, d), lambda b, t: (b, t, 0)),
            pl.BlockSpec((None, N_MOD, nb, d), lambda b, t: (l, 0, n_prompt_blk, 0)),
            pl.BlockSpec((None, 2, d), lambda b, t: (l, 0, 0)),
            _resident((d, 2 * D_C), lambda b, t: (0, 0)),
            pl.BlockSpec(b_in.shape, lambda b, t: (0, 0)),
            pl.BlockSpec(lng.shape, lambda b, t: (0, 0)),
            pl.BlockSpec(lnb.shape, lambda b, t: (0, 0)),
            pl.BlockSpec((None, C_HEADS, CHUNK, CHUNK), lambda b, t: (o, 0, 0, 0)),
            pl.BlockSpec((None, CHUNK, D_C), lambda b, t: (o, 0, 0)),
            _resident((D_C, d), lambda b, t: (0, 0)),
            pl.BlockSpec((ts, seq_blk, d), lambda b, t: (0, sblk(b, t), 0)),
            pl.BlockSpec((None, N_MOD // 2, seq_blk, d), lambda b, t: (l, 0, sblk(b, t), 0)),
            pl.BlockSpec((None, ts, ts, D_C), lambda b, t: (o, 0, 0, 0)),
        ] + c_in,
        out_specs=[
            pl.BlockSpec((None, ROW_TILE, d), lambda b, t: (b, t, 0)),
            pl.BlockSpec((None, CHUNK, D_C), lambda b, t: (b, 0, 0)),
            pl.BlockSpec((ts, seq_blk, d), lambda b, t: (0, sblk(b, t), 0)),
            pl.BlockSpec((ts, seq_blk, D_C), lambda b, t: (0, sblk(b, t), 0)),
        ] + c_out,
        out_shape=[
            jax.ShapeDtypeStruct(x.shape, F32),
            jax.ShapeDtypeStruct((nb, CHUNK, D_C), F32),
            jax.ShapeDtypeStruct(xs.shape, F32),
            jax.ShapeDtypeStruct((ts, ns, D_C), F32),
        ] + c_shapes,
        compiler_params=_params(),
        name=f"odd_{l}",
    )(x, mod, norm_g, w_in, b_in, lng, lnb, w_s, sbias, w_out, xs, mod, wm, *c_args)
    return outs[0], outs[1], outs[2], outs[3], outs[4:]


def _odd_sample_body(x_ref, mod_ref, g_ref, w_in_ref, b_in_ref, lng_ref, lnb_ref, wm_ref,
                     sbias_ref, w_out_ref, o_ref, cv_ref, row):
    x = x_ref[...]
    t, s, d = x.shape
    h = _rms_mod(x, g_ref[0:1, :], mod_ref[0][None], mod_ref[1][None])
    z = jax.nn.gelu(_dot(h.reshape(t * s, d).astype(BF16), w_in_ref[...]) + b_in_ref[row:row + 1, :])
    z = z.reshape(t, s, 2 * D_C)
    u = z[:, :, 0:D_C]
    v = _layernorm(z[:, :, D_C:2 * D_C], lng_ref[row:row + 1, :][None], lnb_ref[row:row + 1, :][None])
    cv_ref[...] = v
    gated = []
    for i in range(t):
        s_i = jnp.broadcast_to(sbias_ref[i:i + 1, :], (s, D_C))
        for j in range(i + 1):
            s_i = s_i + wm_ref[i, j:j + 1, :] * v[j]
        gated.append(u[i] * s_i)
    us = jnp.stack(gated, axis=0).reshape(t * s, D_C).astype(BF16)
    y = _dot(us, w_out_ref[...]).reshape(t, s, d)
    o_ref[...] = x + mod_ref[2][None] * y


def kernel(x_prompt, x_sample, state_conv_a, state_conv_b, c_prompt, c_sample, w_in_ab, conv_a_w, conv_a_b, ln_a_g, ln_a_b, conv_b_w, w_out_ab, w_in_c, b_in_c, ln_v_g, ln_v_b, w_s, b_s, w_out_c, w_ada, b_ada, norm_g, w_ff1, w_ff2, final_g):
    dec_seq = x_sample.shape[1]

    mix_in_bf = w_in_ab[0].astype(BF16)
    mix_out_bf = w_out_ab[0].astype(BF16)

    cab3, lnag3, lnab3 = conv_a_b, ln_a_g, ln_a_b
    binc3, lnvg3, lnvb3 = b_in_c, ln_v_g, ln_v_b
    fg2 = final_g.reshape(1, D_MODEL)

    sbias = jnp.repeat(jnp.swapaxes(b_s, 1, 2), C_HEAD_DIM, axis=2)
    wm = jnp.repeat(jnp.transpose(w_s[:, :, :dec_seq, :dec_seq], (0, 2, 3, 1)),
                    C_HEAD_DIM, axis=3)

    mod = _ada_call(jnp.concatenate([c_sample, c_prompt], axis=0), w_ada, b_ada)

    xp = x_prompt
    xs = jnp.transpose(x_sample, (1, 0, 2))
    ha = jnp.transpose(state_conv_a, (0, 2, 1, 3))
    hb = jnp.transpose(state_conv_b, (0, 2, 1, 3))

    a_p, b_p, v_p, a_s, b_s_out, v_s = [], [], [], [], [], []
    for l in range(DEPTH):
        final = l == DEPTH - 1
        ff_casts = [(w_ff1, l), (w_ff2, l)]
        if l % 2 == 0:
            xp, sa, sb, (ff1_bf, ff2_bf) = _even_prompt_call(
                xp, mod, norm_g, mix_in_bf, conv_a_w, cab3, lnag3, lnab3, conv_b_w, mix_out_bf, l, ff_casts)
            a_p.append(sa)
            b_p.append(sb)
            xs, sa, sb = _even_sample_call(xs, mod, norm_g, ha, hb, mix_in_bf, conv_a_w, cab3, lnag3,
                                           lnab3, conv_b_w, mix_out_bf, l)
            a_s.append(jnp.transpose(sa, (1, 0, 2)))
            b_s_out.append(jnp.transpose(sb, (1, 0, 2)))
        else:
            xp, cv, xs, cvs, (ff1_bf, ff2_bf) = _odd_call(
                xp, xs, mod, norm_g, mix_in_bf, binc3, lnvg3, lnvb3, w_s, sbias, wm, mix_out_bf, l, ff_casts)
            v_p.append(cv)
            v_s.append(jnp.transpose(cvs, (1, 0, 2)))
        if final:
            mix_casts = []
        elif l % 2 == 0:
            mix_casts = [(w_in_c, l // 2), (w_out_c, l // 2)]
        else:
            mix_casts = [(w_in_ab, (l + 1) // 2), (w_out_ab, (l + 1) // 2)]
        xp, xs, next_mix = _mlp_call(xp, xs, mod, norm_g, ff1_bf, ff2_bf, fg2, l, final, mix_casts)
        if next_mix:
            mix_in_bf, mix_out_bf = next_mix

    return (xp, jnp.transpose(xs, (1, 0, 2)), jnp.stack(a_p), jnp.stack(a_s), jnp.stack(b_p),
            jnp.stack(b_s_out), jnp.stack(v_p), jnp.stack(v_s))
```

```python
import functools

import jax
import jax.numpy as jnp
from jax import lax
from jax.experimental import pallas as pl
from jax.experimental.pallas import tpu as pltpu

D_MODEL = 1024
DEPTH = 4
D_A = 512
D_B = 512
CONV_A = 31
CONV_B = 3
D_IN_EVEN = 2 * D_A + 3 * D_B
D_C = 1024
C_HEADS = 8
C_HEAD_DIM = 128
CHUNK = 128
D_FF = 4096
N_MOD = 6
EPS = 1e-6

BF16 = jnp.bfloat16
F32 = jnp.float32

VMEM_LIMIT_BYTES = 56 * 1024 * 1024
SUBLANES = 8
ROW_TILE = 1024
ODD_SUB_ROWS = 512
MLP_ROW_TILE = 512
SEQ_BLOCK = 64
FF_CHUNK = 1024
ADA_GROUP = 3
A_HIST_ROWS = 32
B_HIST_ROWS = 8
LANES = 128
CONV_SUB_ROWS = 256
CONV_ROW_BLOCK = 128


def _params():
    return pltpu.CompilerParams(
        dimension_semantics=("arbitrary", "arbitrary"),
        vmem_limit_bytes=VMEM_LIMIT_BYTES)


def _resident(shape, index_map):
    return pl.BlockSpec(shape, index_map, pipeline_mode=pl.Buffered(1))


def _with_weight_casts(body, n_in, n_out, n_cast):
    def wrapped(*refs):
        ins = refs[:n_in]
        cast_src = refs[n_in:n_in + n_cast]
        outs = refs[n_in + n_cast:n_in + n_cast + n_out]
        cast_dst = refs[n_in + n_cast + n_out:n_in + 2 * n_cast + n_out]
        scratch = refs[n_in + 2 * n_cast + n_out:]
        for src, dst in zip(cast_src, cast_dst):
            dst[...] = src[...].astype(BF16)
        body(*ins, *outs, *scratch)
    return wrapped


def _cast_specs(casts, n_outer, n_inner):
    in_specs, out_specs, out_shapes, args = [], [], [], []
    for w, layer in casts:
        _, r, c = w.shape
        rb = r // (n_outer * n_inner)
        in_specs.append(pl.BlockSpec((None, rb, c), lambda b, t, layer=layer: (layer, b * n_inner + t, 0)))
        out_specs.append(pl.BlockSpec((rb, c), lambda b, t: (b * n_inner + t, 0)))
        out_shapes.append(jax.ShapeDtypeStruct((r, c), BF16))
        args.append(w)
    return in_specs, out_specs, out_shapes, args


def _rms(x, g):
    ms = jnp.mean(x * x, axis=-1, keepdims=True)
    return x * lax.rsqrt(ms + EPS) * g


def _rms_mod(x, g, shift, scale):
    ms = jnp.mean(x * x, axis=-1, keepdims=True)
    return x * lax.rsqrt(ms + EPS) * (g * (1.0 + scale)) + shift


def _layernorm(x, g, b):
    mu = jnp.mean(x, axis=-1, keepdims=True)
    xc = x - mu
    var = jnp.mean(xc * xc, axis=-1, keepdims=True)
    return xc * lax.rsqrt(var + EPS) * g + b


def _silu(x):
    return x * jax.nn.sigmoid(x)


def _dot(a, b):
    return jnp.dot(a, b, preferred_element_type=F32)


def _ada_kernel(c_ref, w_ref, b_ref, o_ref):
    c = c_ref[...]
    res = _dot(_silu(c).astype(BF16), w_ref[...].astype(BF16))
    for i in range(ADA_GROUP):
        o_ref[i] = res[:, i * D_MODEL:(i + 1) * D_MODEL] + b_ref[i]


def _ada_call(c_all, w_ada, b_ada, casts):
    n = c_all.shape[0]
    b4 = b_ada.reshape(DEPTH, N_MOD, 1, D_MODEL)
    n_groups = N_MOD // ADA_GROUP
    c_in, c_out, c_shapes, c_args = _cast_specs(casts, DEPTH, n_groups)
    outs = pl.pallas_call(
        _with_weight_casts(_ada_kernel, 3, 1, len(casts)),
        grid=(DEPTH, n_groups),
        in_specs=[
            pl.BlockSpec((n, D_MODEL), lambda l, j: (0, 0)),
            pl.BlockSpec((None, D_MODEL, ADA_GROUP * D_MODEL), lambda l, j: (l, 0, j)),
            pl.BlockSpec((None, ADA_GROUP, 1, D_MODEL), lambda l, j: (l, j, 0, 0)),
        ] + c_in,
        out_specs=[pl.BlockSpec((None, ADA_GROUP, n, D_MODEL), lambda l, j: (l, j, 0, 0))] + c_out,
        out_shape=[jax.ShapeDtypeStruct((DEPTH, N_MOD, n, D_MODEL), F32)] + c_shapes,
        compiler_params=_params(),
        name="adaln",
    )(c_all, w_ada, b4, *c_args)
    return outs[0], outs[1:]


def _mlp_body(x2d, h_bf, gate, w1_ref, w2_ref):
    acc = None
    for j in range(D_FF // FF_CHUNK):
        cols = slice(j * FF_CHUNK, (j + 1) * FF_CHUNK)
        hid = _dot(h_bf, w1_ref[:, cols])
        hid = jnp.square(jnp.maximum(hid, 0.0)).astype(BF16)
        part = _dot(hid, w2_ref[cols, :])
        acc = part if acc is None else acc + part
    return x2d + gate * acc


def _mlp_prompt_kernel(x_ref, mod_ref, g_ref, w1_ref, w2_ref, fg_ref, xs_ref, mods_ref, o_ref, os_ref,
                       *, final):
    b = pl.program_id(0)
    x = x_ref[...]
    shift = mod_ref[3, pl.ds(b, 1), :]
    scale = mod_ref[4, pl.ds(b, 1), :]
    gate = mod_ref[5, pl.ds(b, 1), :]
    h = _rms_mod(x, g_ref[1:2, :], shift, scale).astype(BF16)
    y = _mlp_body(x, h, gate, w1_ref, w2_ref)
    if final:
        y = _rms(y, fg_ref[...])
    o_ref[...] = y

    @pl.when(jnp.logical_and(b == pl.num_programs(0) - 1,
                             pl.program_id(1) == pl.num_programs(1) - 1))
    def _():
        t, s, d = xs_ref.shape
        half = t // 2
        for p in range(0, t, half):
            xs = xs_ref[p:p + half]
            hs = _rms_mod(xs, g_ref[1:2, :], mods_ref[3][None], mods_ref[4][None])
            hs = hs.reshape(half * s, d).astype(BF16)
            gs = jnp.broadcast_to(mods_ref[5][None], (half, s, d)).reshape(half * s, d)
            ys = _mlp_body(xs.reshape(half * s, d), hs, gs, w1_ref, w2_ref)
            if final:
                ys = _rms(ys, fg_ref[...])
            os_ref[p:p + half] = ys.reshape(half, s, d)


def _mlp_call(x, xs, mod, norm_g, w1, w2, final_g, l, final, casts):
    nb, seq, d = x.shape
    ts, ns, _ = xs.shape
    n_prompt_blk = (mod.shape[2] - nb) // nb
    n_tiles = seq // MLP_ROW_TILE
    c_in, c_out, c_shapes, c_args = _cast_specs(casts, nb, n_tiles)
    outs = pl.pallas_call(
        _with_weight_casts(functools.partial(_mlp_prompt_kernel, final=final), 8, 2, len(casts)),
        grid=(nb, n_tiles),
        in_specs=[
            pl.BlockSpec((None, MLP_ROW_TILE, d), lambda b, t: (b, t, 0)),
            pl.BlockSpec((None, N_MOD, nb, d), lambda b, t: (l, 0, n_prompt_blk, 0)),
            pl.BlockSpec((None, 2, d), lambda b, t: (l, 0, 0)),
            _resident((d, D_FF), lambda b, t: (0, 0)),
            _resident((D_FF, d), lambda b, t: (0, 0)),
            pl.BlockSpec((1, d), lambda b, t: (0, 0)),
            _resident((ts, ns, d), lambda b, t: (0, 0, 0)),
            _resident((None, N_MOD, ns, d), lambda b, t: (l, 0, 0, 0)),
        ] + c_in,
        out_specs=[pl.BlockSpec((None, MLP_ROW_TILE, d), lambda b, t: (b, t, 0)),
                   pl.BlockSpec((ts, ns, d), lambda b, t: (0, 0, 0))] + c_out,
        out_shape=[jax.ShapeDtypeStruct(x.shape, F32), jax.ShapeDtypeStruct(xs.shape, F32)] + c_shapes,
        compiler_params=_params(),
        name=f"mlp_{l}",
    )(x, mod, norm_g, w1, w2, final_g, xs, mod, *c_args)
    return outs[0], outs[1], outs[2:]


def _even_prompt_kernel(x_ref, mod_ref, g_ref, w_in_ref, caw_ref, cab_ref, lng_ref, lnb_ref,
                        cbw_ref, w_out_ref, o_ref, sa_ref, sb_ref, abuf, bbuf, shbuf, cbuf):
    b = pl.program_id(0)
    t = pl.program_id(1)
    rows = x_ref.shape[0]
    sub = CONV_SUB_ROWS
    ext = sub + A_HIST_ROWS
    first_a = A_HIST_ROWS - (CONV_A - 1)
    first_b = B_HIST_ROWS - (CONV_B - 1)

    @pl.when(t == 0)
    def _():
        abuf[0:A_HIST_ROWS, :] = jnp.zeros((A_HIST_ROWS, D_A), F32)
        bbuf[0:B_HIST_ROWS, :] = jnp.zeros((B_HIST_ROWS, D_B), F32)

    shift = mod_ref[0, pl.ds(b, 1), :]
    scale = mod_ref[1, pl.ds(b, 1), :]
    gate = mod_ref[2, pl.ds(b, 1), :]

    for j in range(rows // sub):
        r0 = j * sub
        x = x_ref[r0:r0 + sub, :]
        h = _rms_mod(x, g_ref[0:1, :], shift, scale).astype(BF16)
        z = _dot(h, w_in_ref[...])

        abuf[A_HIST_ROWS + r0:A_HIST_ROWS + r0 + sub, :] = (
            z[:, 0:D_A] * jax.nn.sigmoid(z[:, D_A:2 * D_A]))
        a_ext = abuf[r0:r0 + ext, :]
        for s in range(1, SUBLANES):
            shbuf[j % 2, s - 1] = pltpu.roll(a_ext, ext - s, axis=0)
        for c in range(D_A // LANES):
            lanes = slice(c * LANES, (c + 1) * LANES)
            for rb in range(sub // CONV_ROW_BLOCK):
                acc = jnp.broadcast_to(cab_ref[:, lanes], (CONV_ROW_BLOCK, LANES))
                for k in range(CONV_A):
                    q, s = divmod(first_a + k, SUBLANES)
                    start = q * SUBLANES + rb * CONV_ROW_BLOCK
                    if s == 0:
                        tap = abuf[r0 + start:r0 + start + CONV_ROW_BLOCK, lanes]
                    else:
                        tap = shbuf[j % 2, s - 1, start:start + CONV_ROW_BLOCK, lanes]
                    acc = acc + caw_ref[k:k + 1, lanes] * tap
                cbuf[r0 + rb * CONV_ROW_BLOCK:r0 + (rb + 1) * CONV_ROW_BLOCK, lanes] = acc
        a_out = _silu(_layernorm(cbuf[r0:r0 + sub, :], lng_ref[...], lnb_ref[...]))

        bbuf[B_HIST_ROWS + r0:B_HIST_ROWS + r0 + sub, :] = z[:, 4 * D_A:5 * D_A] * z[:, 2 * D_A:3 * D_A]
        accb = cbw_ref[0:1, :] * bbuf[pl.ds(r0 + first_b, sub), :]
        for k in range(1, CONV_B):
            accb = accb + cbw_ref[k:k + 1, :] * bbuf[pl.ds(r0 + first_b + k, sub), :]
        b_out = z[:, 3 * D_A:4 * D_A] * accb

        y = _dot(jnp.concatenate([a_out, b_out], axis=-1).astype(BF16), w_out_ref[...])
        o_ref[r0:r0 + sub, :] = x + gate * y

    @pl.when(t == pl.num_programs(1) - 1)
    def _():
        sa_ref[...] = abuf[pl.ds(rows + first_a, CONV_A - 1), :]
        sb_ref[...] = bbuf[pl.ds(rows + first_b, CONV_B - 1), :]

    abuf[0:A_HIST_ROWS, :] = abuf[rows:rows + A_HIST_ROWS, :]
    bbuf[0:B_HIST_ROWS, :] = bbuf[rows:rows + B_HIST_ROWS, :]


def _even_prompt_call(x, mod, norm_g, w_in, caw, cab, lng, lnb, cbw, w_out, l, casts):
    nb, seq, d = x.shape
    e = l // 2
    n_prompt_blk = (mod.shape[2] - nb) // nb
    n_tiles = seq // ROW_TILE
    c_in, c_out, c_shapes, c_args = _cast_specs(casts, nb, n_tiles)
    vec = lambda width: pl.BlockSpec((None, 1, width), lambda b, t: (e, 0, 0))
    outs = pl.pallas_call(
        _with_weight_casts(_even_prompt_kernel, 10, 3, len(casts)),
        grid=(nb, n_tiles),
        in_specs=[
            pl.BlockSpec((None, ROW_TILE, d), lambda b, t: (b, t, 0)),
            pl.BlockSpec((None, N_MOD, nb, d), lambda b, t: (l, 0, n_prompt_blk, 0)),
            pl.BlockSpec((None, 2, d), lambda b, t: (l, 0, 0)),
            _resident((d, D_IN_EVEN), lambda b, t: (0, 0)),
            pl.BlockSpec((None, CONV_A, D_A), lambda b, t: (e, 0, 0)),
            vec(D_A), vec(D_A), vec(D_A),
            pl.BlockSpec((None, CONV_B, D_B), lambda b, t: (e, 0, 0)),
            _resident((D_A + D_B, d), lambda b, t: (0, 0)),
        ] + c_in,
        out_specs=[
            pl.BlockSpec((None, ROW_TILE, d), lambda b, t: (b, t, 0)),
            pl.BlockSpec((None, CONV_A - 1, D_A), lambda b, t: (b, 0, 0)),
            pl.BlockSpec((None, CONV_B - 1, D_B), lambda b, t: (b, 0, 0)),
        ] + c_out,
        out_shape=[
            jax.ShapeDtypeStruct(x.shape, F32),
            jax.ShapeDtypeStruct((nb, CONV_A - 1, D_A), F32),
            jax.ShapeDtypeStruct((nb, CONV_B - 1, D_B), F32),
        ] + c_shapes,
        scratch_shapes=[
            pltpu.VMEM((ROW_TILE + A_HIST_ROWS, D_A), F32),
            pltpu.VMEM((ROW_TILE + B_HIST_ROWS, D_B), F32),
            pltpu.VMEM((2, SUBLANES - 1, CONV_SUB_ROWS + A_HIST_ROWS, D_A), F32),
            pltpu.VMEM((ROW_TILE, D_A), F32),
        ],
        compiler_params=_params(),
        name=f"even_prompt_{l}",
    )(x, mod, norm_g, w_in, caw, cab, lng, lnb, cbw, w_out, *c_args)
    return outs[0], outs[1], outs[2], outs[3:]


def _even_sample_kernel(x_ref, mod_ref, g_ref, ha_ref, hb_ref, w_in_ref, caw_ref, cab_ref,
                        lng_ref, lnb_ref, cbw_ref, w_out_ref, o_ref, sa_ref, sb_ref):
    x = x_ref[...]
    t, s, d = x.shape
    h = _rms_mod(x, g_ref[0:1, :], mod_ref[0][None], mod_ref[1][None])
    z = _dot(h.reshape(t * s, d).astype(BF16), w_in_ref[...]).reshape(t, s, D_IN_EVEN)

    a = z[:, :, 0:D_A] * jax.nn.sigmoid(z[:, :, D_A:2 * D_A])
    a_ext = jnp.concatenate([ha_ref[...], a], axis=0)
    acc = jnp.broadcast_to(cab_ref[...][None], (t, s, D_A))
    for k in range(CONV_A):
        acc = acc + caw_ref[k:k + 1, :][None] * a_ext[k:k + t]
    a_out = _silu(_layernorm(acc, lng_ref[...][None], lnb_ref[...][None]))
    sa_ref[...] = a_ext[t:]

    bx = z[:, :, 4 * D_A:5 * D_A] * z[:, :, 2 * D_A:3 * D_A]
    b_ext = jnp.concatenate([hb_ref[...], bx], axis=0)
    accb = cbw_ref[0:1, :][None] * b_ext[0:t]
    for k in range(1, CONV_B):
        accb = accb + cbw_ref[k:k + 1, :][None] * b_ext[k:k + t]
    b_out = z[:, :, 3 * D_A:4 * D_A] * accb
    sb_ref[...] = b_ext[t:]

    cat = jnp.concatenate([a_out, b_out], axis=-1).reshape(t * s, D_A + D_B).astype(BF16)
    y = _dot(cat, w_out_ref[...]).reshape(t, s, d)
    o_ref[...] = x + mod_ref[2][None] * y


def _even_sample_call(x, mod, norm_g, hist_a, hist_b, w_in, caw, cab, lng, lnb, cbw, w_out, l):
    t, ns, d = x.shape
    e = l // 2
    vec = lambda width: pl.BlockSpec((None, 1, width), lambda s, _: (e, 0, 0))
    return pl.pallas_call(
        _even_sample_kernel,
        grid=(ns // SEQ_BLOCK, 1),
        in_specs=[
            pl.BlockSpec((t, SEQ_BLOCK, d), lambda s, _: (0, s, 0)),
            pl.BlockSpec((None, N_MOD, SEQ_BLOCK, d), lambda s, _: (l, 0, s, 0)),
            pl.BlockSpec((None, 2, d), lambda s, _: (l, 0, 0)),
            pl.BlockSpec((None, CONV_A - 1, SEQ_BLOCK, D_A), lambda s, _: (e, 0, s, 0)),
            pl.BlockSpec((None, CONV_B - 1, SEQ_BLOCK, D_B), lambda s, _: (e, 0, s, 0)),
            _resident((d, D_IN_EVEN), lambda s, _: (0, 0)),
            pl.BlockSpec((None, CONV_A, D_A), lambda s, _: (e, 0, 0)),
            vec(D_A), vec(D_A), vec(D_A),
            pl.BlockSpec((None, CONV_B, D_B), lambda s, _: (e, 0, 0)),
            _resident((D_A + D_B, d), lambda s, _: (0, 0)),
        ],
        out_specs=[
            pl.BlockSpec((t, SEQ_BLOCK, d), lambda s, _: (0, s, 0)),
            pl.BlockSpec((CONV_A - 1, SEQ_BLOCK, D_A), lambda s, _: (0, s, 0)),
            pl.BlockSpec((CONV_B - 1, SEQ_BLOCK, D_B), lambda s, _: (0, s, 0)),
        ],
        out_shape=[
            jax.ShapeDtypeStruct(x.shape, F32),
            jax.ShapeDtypeStruct((CONV_A - 1, ns, D_A), F32),
            jax.ShapeDtypeStruct((CONV_B - 1, ns, D_B), F32),
        ],
        compiler_params=_params(),
        name=f"even_sample_{l}",
    )(x, mod, norm_g, hist_a, hist_b, w_in, caw, cab, lng, lnb, cbw, w_out)


def _odd_prompt_kernel(x_ref, mod_ref, g_ref, w_in_ref, b_in_ref, lng_ref, lnb_ref, ws_ref,
                       sbias_ref, w_out_ref, xs_ref, mods_ref, wm_ref, o_ref, cv_ref, os_ref, cvs_ref):
    b = pl.program_id(0)
    t = pl.program_id(1)

    @pl.when(b == pl.num_programs(0) - 1)
    def _():
        _odd_sample_body(xs_ref, mods_ref, g_ref, w_in_ref, b_in_ref, lng_ref, lnb_ref, wm_ref,
                         sbias_ref, w_out_ref, os_ref, cvs_ref)

    rows = x_ref.shape[0]
    sub = ODD_SUB_ROWS
    shift = mod_ref[0, pl.ds(b, 1), :]
    scale = mod_ref[1, pl.ds(b, 1), :]
    gate = mod_ref[2, pl.ds(b, 1), :]
    causal = (lax.broadcasted_iota(jnp.int32, (CHUNK, CHUNK), 0)
              >= lax.broadcasted_iota(jnp.int32, (CHUNK, CHUNK), 1))
    ws = [jnp.where(causal, ws_ref[hd], 0.0).astype(BF16) for hd in range(C_HEADS)]

    for j in range(rows // sub):
        r0 = j * sub
        x = x_ref[r0:r0 + sub, :]
        h = _rms_mod(x, g_ref[0:1, :], shift, scale).astype(BF16)
        z = jax.nn.gelu(_dot(h, w_in_ref[...]) + b_in_ref[...])
        u = z[:, 0:D_C]
        v = _layernorm(z[:, D_C:2 * D_C], lng_ref[...], lnb_ref[...])

        if r0 + sub == rows:
            @pl.when(t == pl.num_programs(1) - 1)
            def _():
                cv_ref[...] = v[sub - CHUNK:sub, :]

        v_bf = v.astype(BF16)
        s_rows = []
        for c in range(sub // CHUNK):
            heads = [
                _dot(ws[hd], v_bf[c * CHUNK:(c + 1) * CHUNK, hd * C_HEAD_DIM:(hd + 1) * C_HEAD_DIM])
                for hd in range(C_HEADS)
            ]
            s_rows.append(jnp.concatenate(heads, axis=-1) + sbias_ref[...])
        s = jnp.concatenate(s_rows, axis=0)
        y = _dot((u * s).astype(BF16), w_out_ref[...])
        o_ref[r0:r0 + sub, :] = x + gate * y


def _odd_call(x, xs, mod, norm_g, w_in, b_in, lng, lnb, w_s, sbias, wm, w_out, l, casts):
    nb, seq, d = x.shape
    ts, ns, _ = xs.shape
    o = l // 2
    n_prompt_blk = (mod.shape[2] - nb) // nb
    n_tiles = seq // ROW_TILE
    seq_blk = ns // n_tiles
    sblk = lambda b, t: jnp.where(b == nb - 1, t, 0)
    c_in, c_out, c_shapes, c_args = _cast_specs(casts, nb, n_tiles)
    outs = pl.pallas_call(
        _with_weight_casts(_odd_prompt_kernel, 13, 4, len(casts)),
        grid=(nb, n_tiles),
        in_specs=[
            pl.BlockSpec((None, ROW_TILE, d), lambda b, t: (b, t, 0)),
            pl.BlockSpec((None, N_MOD, nb, d), lambda b, t: (l, 0, n_prompt_blk, 0)),
            pl.BlockSpec((None, 2, d), lambda b, t: (l, 0, 0)),
            _resident((d, 2 * D_C), lambda b, t: (0, 0)),
            pl.BlockSpec((None, 1, 2 * D_C), lambda b, t: (o, 0, 0)),
            pl.BlockSpec((None, 1, D_C), lambda b, t: (o, 0, 0)),
            pl.BlockSpec((None, 1, D_C), lambda b, t: (o, 0, 0)),
            pl.BlockSpec((None, C_HEADS, CHUNK, CHUNK), lambda b, t: (o, 0, 0, 0)),
            pl.BlockSpec((None, CHUNK, D_C), lambda b, t: (o, 0, 0)),
            _resident((D_C, d), lambda b, t: (0, 0)),
            pl.BlockSpec((ts, seq_blk, d), lambda b, t: (0, sblk(b, t), 0)),
            pl.BlockSpec((None, N_MOD // 2, seq_blk, d), lambda b, t: (l, 0, sblk(b, t), 0)),
            pl.BlockSpec((None, ts, ts, D_C), lambda b, t: (o, 0, 0, 0)),
        ] + c_in,
        out_specs=[
            pl.BlockSpec((None, ROW_TILE, d), lambda b, t: (b, t, 0)),
            pl.BlockSpec((None, CHUNK, D_C), lambda b, t: (b, 0, 0)),
            pl.BlockSpec((ts, seq_blk, d), lambda b, t: (0, sblk(b, t), 0)),
            pl.BlockSpec((ts, seq_blk, D_C), lambda b, t: (0, sblk(b, t), 0)),
        ] + c_out,
        out_shape=[
            jax.ShapeDtypeStruct(x.shape, F32),
            jax.ShapeDtypeStruct((nb, CHUNK, D_C), F32),
            jax.ShapeDtypeStruct(xs.shape, F32),
            jax.ShapeDtypeStruct((ts, ns, D_C), F32),
        ] + c_shapes,
        compiler_params=_params(),
        name=f"odd_{l}",
    )(x, mod, norm_g, w_in, b_in, lng, lnb, w_s, sbias, w_out, xs, mod, wm, *c_args)
    return outs[0], outs[1], outs[2], outs[3], outs[4:]


def _odd_sample_body(x_ref, mod_ref, g_ref, w_in_ref, b_in_ref, lng_ref, lnb_ref, wm_ref,
                     sbias_ref, w_out_ref, o_ref, cv_ref):
    x = x_ref[...]
    t, s, d = x.shape
    h = _rms_mod(x, g_ref[0:1, :], mod_ref[0][None], mod_ref[1][None])
    z = jax.nn.gelu(_dot(h.reshape(t * s, d).astype(BF16), w_in_ref[...]) + b_in_ref[...])
    z = z.reshape(t, s, 2 * D_C)
    u = z[:, :, 0:D_C]
    v = _layernorm(z[:, :, D_C:2 * D_C], lng_ref[...][None], lnb_ref[...][None])
    cv_ref[...] = v
    gated = []
    for i in range(t):
        s_i = jnp.broadcast_to(sbias_ref[i:i + 1, :], (s, D_C))
        for j in range(i + 1):
            s_i = s_i + wm_ref[i, j:j + 1, :] * v[j]
        gated.append(u[i] * s_i)
    us = jnp.stack(gated, axis=0).reshape(t * s, D_C).astype(BF16)
    y = _dot(us, w_out_ref[...]).reshape(t, s, d)
    o_ref[...] = x + mod_ref[2][None] * y


def kernel(x_prompt, x_sample, state_conv_a, state_conv_b, c_prompt, c_sample, w_in_ab, conv_a_w, conv_a_b, ln_a_g, ln_a_b, conv_b_w, w_out_ab, w_in_c, b_in_c, ln_v_g, ln_v_b, w_s, b_s, w_out_c, w_ada, b_ada, norm_g, w_ff1, w_ff2, final_g):
    dec_seq = x_sample.shape[1]
    n_even, n_odd = w_in_ab.shape[0], w_in_c.shape[0]


    cab3 = conv_a_b.reshape(n_even, 1, D_A)
    lnag3 = ln_a_g.reshape(n_even, 1, D_A)
    lnab3 = ln_a_b.reshape(n_even, 1, D_A)
    binc3 = b_in_c.reshape(n_odd, 1, 2 * D_C)
    lnvg3 = ln_v_g.reshape(n_odd, 1, D_C)
    lnvb3 = ln_v_b.reshape(n_odd, 1, D_C)
    fg2 = final_g.reshape(1, D_MODEL)

    sbias = jnp.repeat(jnp.swapaxes(b_s, 1, 2), C_HEAD_DIM, axis=2)
    wm = jnp.repeat(jnp.transpose(w_s[:, :, :dec_seq, :dec_seq], (0, 2, 3, 1)),
                    C_HEAD_DIM, axis=3)

    mod, (mix_in_bf, mix_out_bf) = _ada_call(jnp.concatenate([c_sample, c_prompt], axis=0), w_ada, b_ada,
                                             [(w_in_ab, 0), (w_out_ab, 0)])

    xp = x_prompt
    xs = jnp.transpose(x_sample, (1, 0, 2))
    ha = jnp.transpose(state_conv_a, (0, 2, 1, 3))
    hb = jnp.transpose(state_conv_b, (0, 2, 1, 3))

    a_p, b_p, v_p, a_s, b_s_out, v_s = [], [], [], [], [], []
    for l in range(DEPTH):
        final = l == DEPTH - 1
        ff_casts = [(w_ff1, l), (w_ff2, l)]
        if l % 2 == 0:
            xp, sa, sb, (ff1_bf, ff2_bf) = _even_prompt_call(
                xp, mod, norm_g, mix_in_bf, conv_a_w, cab3, lnag3, lnab3, conv_b_w, mix_out_bf, l, ff_casts)
            a_p.append(sa)
            b_p.append(sb)
            xs, sa, sb = _even_sample_call(xs, mod, norm_g, ha, hb, mix_in_bf, conv_a_w, cab3, lnag3,
                                           lnab3, conv_b_w, mix_out_bf, l)
            a_s.append(jnp.transpose(sa, (1, 0, 2)))
            b_s_out.append(jnp.transpose(sb, (1, 0, 2)))
        else:
            xp, cv, xs, cvs, (ff1_bf, ff2_bf) = _odd_call(
                xp, xs, mod, norm_g, mix_in_bf, binc3, lnvg3, lnvb3, w_s, sbias, wm, mix_out_bf, l, ff_casts)
            v_p.append(cv)
            v_s.append(jnp.transpose(cvs, (1, 0, 2)))
        if final:
            mix_casts = []
        elif l % 2 == 0:
            mix_casts = [(w_in_c, l // 2), (w_out_c, l // 2)]
        else:
            mix_casts = [(w_in_ab, (l + 1) // 2), (w_out_ab, (l + 1) // 2)]
        xp, xs, next_mix = _mlp_call(xp, xs, mod, norm_g, ff1_bf, ff2_bf, fg2, l, final, mix_casts)
        if next_mix:
            mix_in_bf, mix_out_bf = next_mix

    return (xp, jnp.transpose(xs, (1, 0, 2)), jnp.stack(a_p), jnp.stack(a_s), jnp.stack(b_p),
            jnp.stack(b_s_out), jnp.stack(v_p), jnp.stack(v_s))
```

```python
import functools

import jax
import jax.numpy as jnp
from jax import lax
from jax.experimental import pallas as pl
from jax.experimental.pallas import tpu as pltpu

D_MODEL = 1024
DEPTH = 4
D_A = 512
D_B = 512
CONV_A = 31
CONV_B = 3
D_IN_EVEN = 2 * D_A + 3 * D_B
D_C = 1024
C_HEADS = 8
C_HEAD_DIM = 128
CHUNK = 128
D_FF = 4096
N_MOD = 6
EPS = 1e-6

BF16 = jnp.bfloat16
F32 = jnp.float32

VMEM_LIMIT_BYTES = 56 * 1024 * 1024
SUBLANES = 8
ROW_TILE = 1024
ODD_SUB_ROWS = 512
MLP_ROW_TILE = 512
SEQ_BLOCK = 64
FF_CHUNK = 1024
ADA_GROUP = 3
A_HIST_ROWS = 32
B_HIST_ROWS = 8
LANES = 128
CONV_SUB_ROWS = 256
CONV_ROW_BLOCK = 128


def _params(allow_input_fusion=None):
    return pltpu.CompilerParams(
        dimension_semantics=("arbitrary", "arbitrary"),
        allow_input_fusion=allow_input_fusion,
        vmem_limit_bytes=VMEM_LIMIT_BYTES)


def _resident(shape, index_map):
    return pl.BlockSpec(shape, index_map, pipeline_mode=pl.Buffered(1))


def _with_weight_casts(body, n_in, n_out, n_cast):
    def wrapped(*refs):
        ins = refs[:n_in]
        cast_src = refs[n_in:n_in + n_cast]
        outs = refs[n_in + n_cast:n_in + n_cast + n_out]
        cast_dst = refs[n_in + n_cast + n_out:n_in + 2 * n_cast + n_out]
        scratch = refs[n_in + 2 * n_cast + n_out:]
        for src, dst in zip(cast_src, cast_dst):
            dst[...] = src[...].astype(BF16)
        body(*ins, *outs, *scratch)
    return wrapped


def _cast_specs(casts, n_outer, n_inner):
    in_specs, out_specs, out_shapes, args = [], [], [], []
    for w, layer in casts:
        _, r, c = w.shape
        rb = r // (n_outer * n_inner)
        in_specs.append(pl.BlockSpec((None, rb, c), lambda b, t, layer=layer: (layer, b * n_inner + t, 0)))
        out_specs.append(pl.BlockSpec((rb, c), lambda b, t: (b * n_inner + t, 0)))
        out_shapes.append(jax.ShapeDtypeStruct((r, c), BF16))
        args.append(w)
    return in_specs, out_specs, out_shapes, args


def _rms(x, g):
    ms = jnp.mean(x * x, axis=-1, keepdims=True)
    return x * lax.rsqrt(ms + EPS) * g


def _rms_mod(x, g, shift, scale):
    ms = jnp.mean(x * x, axis=-1, keepdims=True)
    return x * lax.rsqrt(ms + EPS) * (g * (1.0 + scale)) + shift


def _layernorm(x, g, b):
    mu = jnp.mean(x, axis=-1, keepdims=True)
    xc = x - mu
    var = jnp.mean(xc * xc, axis=-1, keepdims=True)
    return xc * lax.rsqrt(var + EPS) * g + b


def _silu(x):
    return x * jax.nn.sigmoid(x)


def _dot(a, b):
    return jnp.dot(a, b, preferred_element_type=F32)


def _ada_kernel(c_ref, w_ref, b_ref, o_ref):
    c = c_ref[...]
    res = _dot(_silu(c).astype(BF16), w_ref[...].astype(BF16))
    for i in range(ADA_GROUP):
        o_ref[i] = res[:, i * D_MODEL:(i + 1) * D_MODEL] + b_ref[i]


def _ada_call(c_all, w_ada, b_ada, casts):
    n = c_all.shape[0]
    b4 = b_ada.reshape(DEPTH, N_MOD, 1, D_MODEL)
    n_groups = N_MOD // ADA_GROUP
    c_in, c_out, c_shapes, c_args = _cast_specs(casts, DEPTH, n_groups)
    outs = pl.pallas_call(
        _with_weight_casts(_ada_kernel, 3, 1, len(casts)),
        grid=(DEPTH, n_groups),
        in_specs=[
            pl.BlockSpec((n, D_MODEL), lambda l, j: (0, 0)),
            pl.BlockSpec((None, D_MODEL, ADA_GROUP * D_MODEL), lambda l, j: (l, 0, j)),
            pl.BlockSpec((None, ADA_GROUP, 1, D_MODEL), lambda l, j: (l, j, 0, 0)),
        ] + c_in,
        out_specs=[pl.BlockSpec((None, ADA_GROUP, n, D_MODEL), lambda l, j: (l, j, 0, 0))] + c_out,
        out_shape=[jax.ShapeDtypeStruct((DEPTH, N_MOD, n, D_MODEL), F32)] + c_shapes,
        compiler_params=_params(),
        name="adaln",
    )(c_all, w_ada, b4, *c_args)
    return outs[0], outs[1:]


def _mlp_body(x2d, h_bf, gate, w1_ref, w2_ref):
    acc = None
    for j in range(D_FF // FF_CHUNK):
        cols = slice(j * FF_CHUNK, (j + 1) * FF_CHUNK)
        hid = _dot(h_bf, w1_ref[:, cols])
        hid = jnp.square(jnp.maximum(hid, 0.0)).astype(BF16)
        part = _dot(hid, w2_ref[cols, :])
        acc = part if acc is None else acc + part
    return x2d + gate * acc


def _mlp_prompt_kernel(x_ref, mod_ref, g_ref, w1_ref, w2_ref, fg_ref, xs_ref, mods_ref, o_ref, os_ref,
                       *, final):
    b = pl.program_id(0)
    x = x_ref[...]
    shift = mod_ref[3, pl.ds(b, 1), :]
    scale = mod_ref[4, pl.ds(b, 1), :]
    gate = mod_ref[5, pl.ds(b, 1), :]
    h = _rms_mod(x, g_ref[1:2, :], shift, scale).astype(BF16)
    y = _mlp_body(x, h, gate, w1_ref, w2_ref)
    if final:
        y = _rms(y, fg_ref[...])
    o_ref[...] = y

    @pl.when(jnp.logical_and(b == pl.num_programs(0) - 1,
                             pl.program_id(1) == pl.num_programs(1) - 1))
    def _():
        t, s, d = xs_ref.shape
        half = t // 2
        for p in range(0, t, half):
            xs = xs_ref[p:p + half]
            hs = _rms_mod(xs, g_ref[1:2, :], mods_ref[3][None], mods_ref[4][None])
            hs = hs.reshape(half * s, d).astype(BF16)
            gs = jnp.broadcast_to(mods_ref[5][None], (half, s, d)).reshape(half * s, d)
            ys = _mlp_body(xs.reshape(half * s, d), hs, gs, w1_ref, w2_ref)
            if final:
                ys = _rms(ys, fg_ref[...])
            os_ref[p:p + half] = ys.reshape(half, s, d)


def _mlp_call(x, xs, mod, norm_g, w1, w2, final_g, l, final, casts):
    nb, seq, d = x.shape
    ts, ns, _ = xs.shape
    n_prompt_blk = (mod.shape[2] - nb) // nb
    n_tiles = seq // MLP_ROW_TILE
    c_in, c_out, c_shapes, c_args = _cast_specs(casts, nb, n_tiles)
    outs = pl.pallas_call(
        _with_weight_casts(functools.partial(_mlp_prompt_kernel, final=final), 8, 2, len(casts)),
        grid=(nb, n_tiles),
        in_specs=[
            pl.BlockSpec((None, MLP_ROW_TILE, d), lambda b, t: (b, t, 0)),
            pl.BlockSpec((None, N_MOD, nb, d), lambda b, t: (l, 0, n_prompt_blk, 0)),
            pl.BlockSpec((None, 2, d), lambda b, t: (l, 0, 0)),
            _resident((d, D_FF), lambda b, t: (0, 0)),
            _resident((D_FF, d), lambda b, t: (0, 0)),
            pl.BlockSpec((1, d), lambda b, t: (0, 0)),
            _resident((ts, ns, d), lambda b, t: (0, 0, 0)),
            _resident((None, N_MOD, ns, d), lambda b, t: (l, 0, 0, 0)),
        ] + c_in,
        out_specs=[pl.BlockSpec((None, MLP_ROW_TILE, d), lambda b, t: (b, t, 0)),
                   pl.BlockSpec((ts, ns, d), lambda b, t: (0, 0, 0))] + c_out,
        out_shape=[jax.ShapeDtypeStruct(x.shape, F32), jax.ShapeDtypeStruct(xs.shape, F32)] + c_shapes,
        compiler_params=_params(),
        name=f"mlp_{l}",
    )(x, mod, norm_g, w1, w2, final_g, xs, mod, *c_args)
    return outs[0], outs[1], outs[2:]


def _even_prompt_kernel(x_ref, mod_ref, g_ref, w_in_ref, caw_ref, cab_ref, lng_ref, lnb_ref,
                        cbw_ref, w_out_ref, o_ref, sa_ref, sb_ref, abuf, bbuf, shbuf, cbuf):
    b = pl.program_id(0)
    t = pl.program_id(1)
    rows = x_ref.shape[0]
    sub = CONV_SUB_ROWS
    ext = sub + A_HIST_ROWS
    first_a = A_HIST_ROWS - (CONV_A - 1)
    first_b = B_HIST_ROWS - (CONV_B - 1)

    @pl.when(t == 0)
    def _():
        abuf[0:A_HIST_ROWS, :] = jnp.zeros((A_HIST_ROWS, D_A), F32)
        bbuf[0:B_HIST_ROWS, :] = jnp.zeros((B_HIST_ROWS, D_B), F32)

    shift = mod_ref[0, pl.ds(b, 1), :]
    scale = mod_ref[1, pl.ds(b, 1), :]
    gate = mod_ref[2, pl.ds(b, 1), :]

    for j in range(rows // sub):
        r0 = j * sub
        x = x_ref[r0:r0 + sub, :]
        h = _rms_mod(x, g_ref[0:1, :], shift, scale).astype(BF16)
        z = _dot(h, w_in_ref[...])

        abuf[A_HIST_ROWS + r0:A_HIST_ROWS + r0 + sub, :] = (
            z[:, 0:D_A] * jax.nn.sigmoid(z[:, D_A:2 * D_A]))
        a_ext = abuf[r0:r0 + ext, :]
        for s in range(1, SUBLANES):
            shbuf[j % 2, s - 1] = pltpu.roll(a_ext, ext - s, axis=0)
        for c in range(D_A // LANES):
            lanes = slice(c * LANES, (c + 1) * LANES)
            for rb in range(sub // CONV_ROW_BLOCK):
                acc = jnp.broadcast_to(cab_ref[:, lanes], (CONV_ROW_BLOCK, LANES))
                for k in range(CONV_A):
                    q, s = divmod(first_a + k, SUBLANES)
                    start = q * SUBLANES + rb * CONV_ROW_BLOCK
                    if s == 0:
                        tap = abuf[r0 + start:r0 + start + CONV_ROW_BLOCK, lanes]
                    else:
                        tap = shbuf[j % 2, s - 1, start:start + CONV_ROW_BLOCK, lanes]
                    acc = acc + caw_ref[k:k + 1, lanes] * tap
                cbuf[r0 + rb * CONV_ROW_BLOCK:r0 + (rb + 1) * CONV_ROW_BLOCK, lanes] = acc
        a_out = _silu(_layernorm(cbuf[r0:r0 + sub, :], lng_ref[...], lnb_ref[...]))

        bbuf[B_HIST_ROWS + r0:B_HIST_ROWS + r0 + sub, :] = z[:, 4 * D_A:5 * D_A] * z[:, 2 * D_A:3 * D_A]
        accb = cbw_ref[0:1, :] * bbuf[pl.ds(r0 + first_b, sub), :]
        for k in range(1, CONV_B):
            accb = accb + cbw_ref[k:k + 1, :] * bbuf[pl.ds(r0 + first_b + k, sub), :]
        b_out = z[:, 3 * D_A:4 * D_A] * accb

        y = _dot(jnp.concatenate([a_out, b_out], axis=-1).astype(BF16), w_out_ref[...])
        o_ref[r0:r0 + sub, :] = x + gate * y

    @pl.when(t == pl.num_programs(1) - 1)
    def _():
        sa_ref[...] = abuf[pl.ds(rows + first_a, CONV_A - 1), :]
        sb_ref[...] = bbuf[pl.ds(rows + first_b, CONV_B - 1), :]

    abuf[0:A_HIST_ROWS, :] = abuf[rows:rows + A_HIST_ROWS, :]
    bbuf[0:B_HIST_ROWS, :] = bbuf[rows:rows + B_HIST_ROWS, :]


def _even_prompt_call(x, mod, norm_g, w_in, caw, cab, lng, lnb, cbw, w_out, l, casts):
    nb, seq, d = x.shape
    e = l // 2
    n_prompt_blk = (mod.shape[2] - nb) // nb
    n_tiles = seq // ROW_TILE
    c_in, c_out, c_shapes, c_args = _cast_specs(casts, nb, n_tiles)
    vec = lambda width: pl.BlockSpec((None, 1, width), lambda b, t: (e, 0, 0))
    outs = pl.pallas_call(
        _with_weight_casts(_even_prompt_kernel, 10, 3, len(casts)),
        grid=(nb, n_tiles),
        in_specs=[
            pl.BlockSpec((None, ROW_TILE, d), lambda b, t: (b, t, 0)),
            pl.BlockSpec((None, N_MOD, nb, d), lambda b, t: (l, 0, n_prompt_blk, 0)),
            pl.BlockSpec((None, 2, d), lambda b, t: (l, 0, 0)),
            _resident((d, D_IN_EVEN), lambda b, t: (0, 0)),
            pl.BlockSpec((None, CONV_A, D_A), lambda b, t: (e, 0, 0)),
            vec(D_A), vec(D_A), vec(D_A),
            pl.BlockSpec((None, CONV_B, D_B), lambda b, t: (e, 0, 0)),
            _resident((D_A + D_B, d), lambda b, t: (0, 0)),
        ] + c_in,
        out_specs=[
            pl.BlockSpec((None, ROW_TILE, d), lambda b, t: (b, t, 0)),
            pl.BlockSpec((None, CONV_A - 1, D_A), lambda b, t: (b, 0, 0)),
            pl.BlockSpec((None, CONV_B - 1, D_B), lambda b, t: (b, 0, 0)),
        ] + c_out,
        out_shape=[
            jax.ShapeDtypeStruct(x.shape, F32),
            jax.ShapeDtypeStruct((nb, CONV_A - 1, D_A), F32),
            jax.ShapeDtypeStruct((nb, CONV_B - 1, D_B), F32),
        ] + c_shapes,
        scratch_shapes=[
            pltpu.VMEM((ROW_TILE + A_HIST_ROWS, D_A), F32),
            pltpu.VMEM((ROW_TILE + B_HIST_ROWS, D_B), F32),
            pltpu.VMEM((2, SUBLANES - 1, CONV_SUB_ROWS + A_HIST_ROWS, D_A), F32),
            pltpu.VMEM((ROW_TILE, D_A), F32),
        ],
        compiler_params=_params(),
        name=f"even_prompt_{l}",
    )(x, mod, norm_g, w_in, caw, cab, lng, lnb, cbw, w_out, *c_args)
    return outs[0], outs[1], outs[2], outs[3:]


def _even_sample_kernel(x_ref, mod_ref, g_ref, ha_ref, hb_ref, w_in_ref, caw_ref, cab_ref,
                        lng_ref, lnb_ref, cbw_ref, w_out_ref, o_ref, sa_ref, sb_ref):
    x = x_ref[...]
    t, s, d = x.shape
    h = _rms_mod(x, g_ref[0:1, :], mod_ref[0][None], mod_ref[1][None])
    z = _dot(h.reshape(t * s, d).astype(BF16), w_in_ref[...]).reshape(t, s, D_IN_EVEN)

    a = z[:, :, 0:D_A] * jax.nn.sigmoid(z[:, :, D_A:2 * D_A])
    a_ext = jnp.concatenate([ha_ref[...], a], axis=0)
    acc = jnp.broadcast_to(cab_ref[...][None], (t, s, D_A))
    for k in range(CONV_A):
        acc = acc + caw_ref[k:k + 1, :][None] * a_ext[k:k + t]
    a_out = _silu(_layernorm(acc, lng_ref[...][None], lnb_ref[...][None]))
    sa_ref[...] = a_ext[t:]

    bx = z[:, :, 4 * D_A:5 * D_A] * z[:, :, 2 * D_A:3 * D_A]
    b_ext = jnp.concatenate([hb_ref[...], bx], axis=0)
    accb = cbw_ref[0:1, :][None] * b_ext[0:t]
    for k in range(1, CONV_B):
        accb = accb + cbw_ref[k:k + 1, :][None] * b_ext[k:k + t]
    b_out = z[:, :, 3 * D_A:4 * D_A] * accb
    sb_ref[...] = b_ext[t:]

    cat = jnp.concatenate([a_out, b_out], axis=-1).reshape(t * s, D_A + D_B).astype(BF16)
    y = _dot(cat, w_out_ref[...]).reshape(t, s, d)
    o_ref[...] = x + mod_ref[2][None] * y


def _even_sample_call(x, mod, norm_g, hist_a, hist_b, w_in, caw, cab, lng, lnb, cbw, w_out, l):
    t, ns, d = x.shape
    e = l // 2
    vec = lambda width: pl.BlockSpec((None, 1, width), lambda s, _: (e, 0, 0))
    return pl.pallas_call(
        _even_sample_kernel,
        grid=(ns // SEQ_BLOCK, 1),
        in_specs=[
            pl.BlockSpec((t, SEQ_BLOCK, d), lambda s, _: (0, s, 0)),
            pl.BlockSpec((None, N_MOD, SEQ_BLOCK, d), lambda s, _: (l, 0, s, 0)),
            pl.BlockSpec((None, 2, d), lambda s, _: (l, 0, 0)),
            pl.BlockSpec((None, CONV_A - 1, SEQ_BLOCK, D_A), lambda s, _: (e, 0, s, 0)),
            pl.BlockSpec((None, CONV_B - 1, SEQ_BLOCK, D_B), lambda s, _: (e, 0, s, 0)),
            _resident((d, D_IN_EVEN), lambda s, _: (0, 0)),
            pl.BlockSpec((None, CONV_A, D_A), lambda s, _: (e, 0, 0)),
            vec(D_A), vec(D_A), vec(D_A),
            pl.BlockSpec((None, CONV_B, D_B), lambda s, _: (e, 0, 0)),
            _resident((D_A + D_B, d), lambda s, _: (0, 0)),
        ],
        out_specs=[
            pl.BlockSpec((t, SEQ_BLOCK, d), lambda s, _: (0, s, 0)),
            pl.BlockSpec((CONV_A - 1, SEQ_BLOCK, D_A), lambda s, _: (0, s, 0)),
            pl.BlockSpec((CONV_B - 1, SEQ_BLOCK, D_B), lambda s, _: (0, s, 0)),
        ],
        out_shape=[
            jax.ShapeDtypeStruct(x.shape, F32),
            jax.ShapeDtypeStruct((CONV_A - 1, ns, D_A), F32),
            jax.ShapeDtypeStruct((CONV_B - 1, ns, D_B), F32),
        ],
        compiler_params=_params([i in (3, 4) for i in range(12)]),
        name=f"even_sample_{l}",
    )(x, mod, norm_g, hist_a, hist_b, w_in, caw, cab, lng, lnb, cbw, w_out)


def _odd_prompt_kernel(x_ref, mod_ref, g_ref, w_in_ref, b_in_ref, lng_ref, lnb_ref, ws_ref,
                       sbias_ref, w_out_ref, xs_ref, mods_ref, wm_ref, o_ref, cv_ref, os_ref, cvs_ref):
    b = pl.program_id(0)
    t = pl.program_id(1)

    @pl.when(b == pl.num_programs(0) - 1)
    def _():
        _odd_sample_body(xs_ref, mods_ref, g_ref, w_in_ref, b_in_ref, lng_ref, lnb_ref, wm_ref,
                         sbias_ref, w_out_ref, os_ref, cvs_ref)

    rows = x_ref.shape[0]
    sub = ODD_SUB_ROWS
    shift = mod_ref[0, pl.ds(b, 1), :]
    scale = mod_ref[1, pl.ds(b, 1), :]
    gate = mod_ref[2, pl.ds(b, 1), :]
    causal = (lax.broadcasted_iota(jnp.int32, (CHUNK, CHUNK), 0)
              >= lax.broadcasted_iota(jnp.int32, (CHUNK, CHUNK), 1))
    ws = [jnp.where(causal, ws_ref[hd], 0.0).astype(BF16) for hd in range(C_HEADS)]

    for j in range(rows // sub):
        r0 = j * sub
        x = x_ref[r0:r0 + sub, :]
        h = _rms_mod(x, g_ref[0:1, :], shift, scale).astype(BF16)
        z = jax.nn.gelu(_dot(h, w_in_ref[...]) + b_in_ref[...])
        u = z[:, 0:D_C]
        v = _layernorm(z[:, D_C:2 * D_C], lng_ref[...], lnb_ref[...])

        if r0 + sub == rows:
            @pl.when(t == pl.num_programs(1) - 1)
            def _():
                cv_ref[...] = v[sub - CHUNK:sub, :]

        v_bf = v.astype(BF16)
        s_rows = []
        for c in range(sub // CHUNK):
            heads = [
                _dot(ws[hd], v_bf[c * CHUNK:(c + 1) * CHUNK, hd * C_HEAD_DIM:(hd + 1) * C_HEAD_DIM])
                for hd in range(C_HEADS)
            ]
            s_rows.append(jnp.concatenate(heads, axis=-1) + sbias_ref[...])
        s = jnp.concatenate(s_rows, axis=0)
        y = _dot((u * s).astype(BF16), w_out_ref[...])
        o_ref[r0:r0 + sub, :] = x + gate * y


def _odd_call(x, xs, mod, norm_g, w_in, b_in, lng, lnb, w_s, sbias, wm, w_out, l, casts):
    nb, seq, d = x.shape
    ts, ns, _ = xs.shape
    o = l // 2
    n_prompt_blk = (mod.shape[2] - nb) // nb
    n_tiles = seq // ROW_TILE
    seq_blk = ns // n_tiles
    sblk = lambda b, t: jnp.where(b == nb - 1, t, 0)
    c_in, c_out, c_shapes, c_args = _cast_specs(casts, nb, n_tiles)
    outs = pl.pallas_call(
        _with_weight_casts(_odd_prompt_kernel, 13, 4, len(casts)),
        grid=(nb, n_tiles),
        in_specs=[
            pl.BlockSpec((None, ROW_TILE, d), lambda b, t: (b, t, 0)),
            pl.BlockSpec((None, N_MOD, nb, d), lambda b, t: (l, 0, n_prompt_blk, 0)),
            pl.BlockSpec((None, 2, d), lambda b, t: (l, 0, 0)),
            _resident((d, 2 * D_C), lambda b, t: (0, 0)),
            pl.BlockSpec((None, 1, 2 * D_C), lambda b, t: (o, 0, 0)),
            pl.BlockSpec((None, 1, D_C), lambda b, t: (o, 0, 0)),
            pl.BlockSpec((None, 1, D_C), lambda b, t: (o, 0, 0)),
            pl.BlockSpec((None, C_HEADS, CHUNK, CHUNK), lambda b, t: (o, 0, 0, 0)),
            pl.BlockSpec((None, CHUNK, D_C), lambda b, t: (o, 0, 0)),
            _resident((D_C, d), lambda b, t: (0, 0)),
            pl.BlockSpec((ts, seq_blk, d), lambda b, t: (0, sblk(b, t), 0)),
            pl.BlockSpec((None, N_MOD // 2, seq_blk, d), lambda b, t: (l, 0, sblk(b, t), 0)),
            pl.BlockSpec((None, ts, ts, D_C), lambda b, t: (o, 0, 0, 0)),
        ] + c_in,
        out_specs=[
            pl.BlockSpec((None, ROW_TILE, d), lambda b, t: (b, t, 0)),
            pl.BlockSpec((None, CHUNK, D_C), lambda b, t: (b, 0, 0)),
            pl.BlockSpec((ts, seq_blk, d), lambda b, t: (0, sblk(b, t), 0)),
            pl.BlockSpec((ts, seq_blk, D_C), lambda b, t: (0, sblk(b, t), 0)),
        ] + c_out,
        out_shape=[
            jax.ShapeDtypeStruct(x.shape, F32),
            jax.ShapeDtypeStruct((nb, CHUNK, D_C), F32),
            jax.ShapeDtypeStruct(xs.shape, F32),
            jax.ShapeDtypeStruct((ts, ns, D_C), F32),
        ] + c_shapes,
        compiler_params=_params(),
        name=f"odd_{l}",
    )(x, mod, norm_g, w_in, b_in, lng, lnb, w_s, sbias, w_out, xs, mod, wm, *c_args)
    return outs[0], outs[1], outs[2], outs[3], outs[4:]


def _odd_sample_body(x_ref, mod_ref, g_ref, w_in_ref, b_in_ref, lng_ref, lnb_ref, wm_ref,
                     sbias_ref, w_out_ref, o_ref, cv_ref):
    x = x_ref[...]
    t, s, d = x.shape
    h = _rms_mod(x, g_ref[0:1, :], mod_ref[0][None], mod_ref[1][None])
    z = jax.nn.gelu(_dot(h.reshape(t * s, d).astype(BF16), w_in_ref[...]) + b_in_ref[...])
    z = z.reshape(t, s, 2 * D_C)
    u = z[:, :, 0:D_C]
    v = _layernorm(z[:, :, D_C:2 * D_C], lng_ref[...][None], lnb_ref[...][None])
    cv_ref[...] = v
    gated = []
    for i in range(t):
        s_i = jnp.broadcast_to(sbias_ref[i:i + 1, :], (s, D_C))
        for j in range(i + 1):
            s_i = s_i + wm_ref[i, j:j + 1, :] * v[j]
        gated.append(u[i] * s_i)
    us = jnp.stack(gated, axis=0).reshape(t * s, D_C).astype(BF16)
    y = _dot(us, w_out_ref[...]).reshape(t, s, d)
    o_ref[...] = x + mod_ref[2][None] * y


def kernel(x_prompt, x_sample, state_conv_a, state_conv_b, c_prompt, c_sample, w_in_ab, conv_a_w, conv_a_b, ln_a_g, ln_a_b, conv_b_w, w_out_ab, w_in_c, b_in_c, ln_v_g, ln_v_b, w_s, b_s, w_out_c, w_ada, b_ada, norm_g, w_ff1, w_ff2, final_g):
    dec_seq = x_sample.shape[1]
    n_even, n_odd = w_in_ab.shape[0], w_in_c.shape[0]


    cab3 = conv_a_b.reshape(n_even, 1, D_A)
    lnag3 = ln_a_g.reshape(n_even, 1, D_A)
    lnab3 = ln_a_b.reshape(n_even, 1, D_A)
    binc3 = b_in_c.reshape(n_odd, 1, 2 * D_C)
    lnvg3 = ln_v_g.reshape(n_odd, 1, D_C)
    lnvb3 = ln_v_b.reshape(n_odd, 1, D_C)
    fg2 = final_g.reshape(1, D_MODEL)

    sbias = jnp.repeat(jnp.swapaxes(b_s, 1, 2), C_HEAD_DIM, axis=2)
    wm = jnp.repeat(jnp.transpose(w_s[:, :, :dec_seq, :dec_seq], (0, 2, 3, 1)),
                    C_HEAD_DIM, axis=3)

    mod, (mix_in_bf, mix_out_bf) = _ada_call(jnp.concatenate([c_sample, c_prompt], axis=0), w_ada, b_ada,
                                             [(w_in_ab, 0), (w_out_ab, 0)])

    xp = x_prompt
    xs = jnp.transpose(x_sample, (1, 0, 2))
    ha = jnp.transpose(state_conv_a, (0, 2, 1, 3))
    hb = jnp.transpose(state_conv_b, (0, 2, 1, 3))

    a_p, b_p, v_p, a_s, b_s_out, v_s = [], [], [], [], [], []
    for l in range(DEPTH):
        final = l == DEPTH - 1
        ff_casts = [(w_ff1, l), (w_ff2, l)]
        if l % 2 == 0:
            xp, sa, sb, (ff1_bf, ff2_bf) = _even_prompt_call(
                xp, mod, norm_g, mix_in_bf, conv_a_w, cab3, lnag3, lnab3, conv_b_w, mix_out_bf, l, ff_casts)
            a_p.append(sa)
            b_p.append(sb)
            xs, sa, sb = _even_sample_call(xs, mod, norm_g, ha, hb, mix_in_bf, conv_a_w, cab3, lnag3,
                                           lnab3, conv_b_w, mix_out_bf, l)
            a_s.append(jnp.transpose(sa, (1, 0, 2)))
            b_s_out.append(jnp.transpose(sb, (1, 0, 2)))
        else:
            xp, cv, xs, cvs, (ff1_bf, ff2_bf) = _odd_call(
                xp, xs, mod, norm_g, mix_in_bf, binc3, lnvg3, lnvb3, w_s, sbias, wm, mix_out_bf, l, ff_casts)
            v_p.append(cv)
            v_s.append(jnp.transpose(cvs, (1, 0, 2)))
        if final:
            mix_casts = []
        elif l % 2 == 0:
            mix_casts = [(w_in_c, l // 2), (w_out_c, l // 2)]
        else:
            mix_casts = [(w_in_ab, (l + 1) // 2), (w_out_ab, (l + 1) // 2)]
        xp, xs, next_mix = _mlp_call(xp, xs, mod, norm_g, ff1_bf, ff2_bf, fg2, l, final, mix_casts)
        if next_mix:
            mix_in_bf, mix_out_bf = next_mix

    return (xp, jnp.transpose(xs, (1, 0, 2)), jnp.stack(a_p), jnp.stack(a_s), jnp.stack(b_p),
            jnp.stack(b_s_out), jnp.stack(v_p), jnp.stack(v_s))
```
